```python
import math
import jax, jax.numpy as jnp
from jax import lax
import numpy as np

D_MODEL = 1024
BATCH = 2
SEQ = 16384
DEPTH = 1
DEC_BATCH = 4
DEC_SEQ = 8192
PAST_LEN = 128

GRID_W = 64
RET_HEADS = 4
RET_DK = 256
RET_DV = 512
RET_QK_W = RET_HEADS * RET_DK
RET_V_W = RET_HEADS * RET_DV
RET_CHUNK = 128
ROPE_BASE = 10000.0
NA_HEADS = 16
NA_HEAD_DIM = 64
NA_W = NA_HEADS * NA_HEAD_DIM
NA_WIN_ROWS = 8
NA_WIN_COLS = 16
IN_COLS = 2 * RET_QK_W + 2 * RET_V_W + 3 * NA_W + 2 * D_MODEL
N_GROUPS = 4
EXPERTS_PER_GROUP = 4
N_EXPERTS = N_GROUPS * EXPERTS_PER_GROUP
EXPERT_TOP_K = 2
D_FF_EXPERT = 256
DEEPNORM_ALPHA = (2 * DEPTH) ** 0.25
DEEPNORM_BETA = (8 * DEPTH) ** -0.25
LN_EPS = 1e-5
GN_EPS = 1e-5

kernel_name = 'hybrid_retention_natten_hmoe_encoder'


def _layer_norm(x, g, b):
    xf = x.astype(jnp.float32)
    mu = xf.mean(-1, keepdims=True)
    var = jnp.square(xf - mu).mean(-1, keepdims=True)
    y = (xf - mu) * lax.rsqrt(var + LN_EPS) * g.astype(jnp.float32) + b.astype(jnp.float32)
    return y.astype(x.dtype)


def _split_in_proj(x, w_in):
    widths = (RET_QK_W, RET_QK_W, RET_V_W, RET_V_W, NA_W, NA_W, NA_W, D_MODEL, D_MODEL)
    outs = []
    start = 0
    for w in widths:
        outs.append(x @ w_in[:, start:start + w])
        start += w
    return outs


def _rotary(x):
    T = x.shape[1]
    half = RET_DK // 2
    inv_freq = ROPE_BASE ** (-jnp.arange(half, dtype=jnp.float32) / half)
    ang = jnp.arange(T, dtype=jnp.float32)[:, None] * inv_freq[None, :]
    cos = jnp.cos(ang)[None, :, None, :]
    sin = jnp.sin(ang)[None, :, None, :]
    x1, x2 = x[..., :half], x[..., half:]
    return jnp.concatenate([x1 * cos - x2 * sin, x1 * sin + x2 * cos], axis=-1)


def _retention_direction(q, k, v, log_gamma, include_diag):
    B, T, H, DK = q.shape
    DV = v.shape[-1]
    nc = T // RET_CHUNK
    qc = q.reshape(B, nc, RET_CHUNK, H, DK)
    kc = k.reshape(B, nc, RET_CHUNK, H, DK)
    vc = v.reshape(B, nc, RET_CHUNK, H, DV)
    pos = jnp.arange(RET_CHUNK, dtype=jnp.float32)
    diff = pos[:, None] - pos[None, :]
    mask = (diff >= 0) if include_diag else (diff > 0)
    intra_decay = jnp.where(mask[None], jnp.exp(jnp.maximum(diff, 0.0)[None] * log_gamma[:, None, None]), 0.0)
    scores = jnp.einsum('bnihk,bnjhk->bnhij', qc, kc) * intra_decay
    intra = jnp.einsum('bnhij,bnjhv->bnihv', scores, vc)
    q_decay = jnp.exp((pos[:, None] + 1.0) * log_gamma[None, :])
    k_decay = jnp.exp((RET_CHUNK - 1.0 - pos)[:, None] * log_gamma[None, :])
    chunk_decay = jnp.exp(RET_CHUNK * log_gamma)

    def step(state, xs):
        q_n, k_n, v_n = xs
        cross = jnp.einsum('bihk,bhkv->bihv', q_n, state) * q_decay[None, :, :, None]
        state = state * chunk_decay[None, :, None, None] + jnp.einsum(
            'bjhk,bjhv->bhkv', k_n * k_decay[None, :, :, None], v_n)
        return state, cross

    state0 = jnp.zeros((B, H, DK, DV), jnp.float32)
    xs = (qc.transpose(1, 0, 2, 3, 4), kc.transpose(1, 0, 2, 3, 4), vc.transpose(1, 0, 2, 3, 4))
    _, cross = lax.scan(step, state0, xs)
    return (intra + cross.transpose(1, 0, 2, 3, 4)).reshape(B, T, H, DV)


def _retention_branch(q, k, v, g, decay_fwd_logit, decay_bwd_logit):
    B, T, _ = q.shape
    f32 = jnp.float32
    qh = _rotary(q.reshape(B, T, RET_HEADS, RET_DK).astype(f32))
    kh = _rotary(k.reshape(B, T, RET_HEADS, RET_DK).astype(f32)) * (RET_DK ** -0.5)
    vh = v.reshape(B, T, RET_HEADS, RET_DV).astype(f32)
    lg_fwd = jax.nn.log_sigmoid(decay_fwd_logit.astype(f32))
    lg_bwd = jax.nn.log_sigmoid(decay_bwd_logit.astype(f32))
    o_fwd = _retention_direction(qh, kh, vh, lg_fwd, True)
    o_bwd = _retention_direction(qh[:, ::-1], kh[:, ::-1], vh[:, ::-1], lg_bwd, False)[:, ::-1]
    o = o_fwd + o_bwd
    mu = o.mean(-1, keepdims=True)
    var = jnp.square(o - mu).mean(-1, keepdims=True)
    o = ((o - mu) * lax.rsqrt(var + GN_EPS)).reshape(B, T, RET_V_W)
    return (jax.nn.silu(g.astype(f32)) * o).astype(g.dtype)


def _neighbourhood_attention(q, k, v, rpb):
    B, T, _ = q.shape
    rows = T // GRID_W
    win_r = min(NA_WIN_ROWS, rows)
    grid = (B, rows, GRID_W, NA_HEADS, NA_HEAD_DIM)
    qg, kg, vg = q.reshape(grid), k.reshape(grid), v.reshape(grid)
    cols = np.arange(GRID_W)
    col_start = np.clip(cols - NA_WIN_COLS // 2, 0, GRID_W - NA_WIN_COLS)
    col_idx = col_start[:, None] + np.arange(NA_WIN_COLS)[None, :]
    col_bias = rpb[:, :, col_idx - cols[:, None] + NA_WIN_COLS - 1]
    scale = NA_HEAD_DIM ** -0.5

    def row_block(r):
        r0 = jnp.clip(r - win_r // 2, 0, rows - win_r)
        q_row = lax.dynamic_index_in_dim(qg, r, axis=1, keepdims=False)
        k_rows = lax.dynamic_slice_in_dim(kg, r0, win_r, axis=1)
        v_rows = lax.dynamic_slice_in_dim(vg, r0, win_r, axis=1)
        k_win = k_rows[:, :, col_idx]
        v_win = v_rows[:, :, col_idx]
        dr = r0 + jnp.arange(win_r) - r + NA_WIN_ROWS - 1
        bias = jnp.take(col_bias, dr, axis=1).transpose(0, 2, 1, 3)
        s = jnp.einsum('bchd,bicjhd->bhcij', q_row, k_win).astype(jnp.float32) * scale
        s = s + bias[None].astype(jnp.float32)
        p = jax.nn.softmax(s.reshape(B, NA_HEADS, GRID_W, win_r * NA_WIN_COLS), axis=-1)
        p = p.reshape(B, NA_HEADS, GRID_W, win_r, NA_WIN_COLS).astype(v.dtype)
        return jnp.einsum('bhcij,bicjhd->bchd', p, v_win)

    out = lax.map(row_block, jnp.arange(rows))
    return out.transpose(1, 0, 2, 3, 4).reshape(B, T, NA_W)


def _hierarchical_moe(x, wg, bg, we, be, w1, w3, w2):
    B, T, D = x.shape
    f32 = jnp.float32
    xf = x.reshape(B * T, D)
    group_prob = jax.nn.softmax((xf @ wg).astype(f32) + bg.astype(f32), axis=-1)
    group_p, group_idx = lax.top_k(group_prob, 1)
    exp_logits = jnp.einsum('nd,dge->nge', xf, we).astype(f32) + be.astype(f32)
    exp_logits = jnp.einsum('nge,ng->ne', exp_logits, jax.nn.one_hot(group_idx[:, 0], N_GROUPS, dtype=f32))
    top_val, top_idx = lax.top_k(exp_logits, EXPERT_TOP_K)
    weights = jax.nn.softmax(top_val, axis=-1) * group_p
    flat_idx = group_idx * EXPERTS_PER_GROUP + top_idx
    gate = jnp.einsum('nk,nke->ne', weights, jax.nn.one_hot(flat_idx, N_EXPERTS, dtype=f32))
    y = jnp.zeros((B * T, D), f32)
    for e in range(N_EXPERTS):
        h = jax.nn.silu(xf @ w1[e]) * (xf @ w3[e])
        y = y + gate[:, e:e + 1] * (h @ w2[e]).astype(f32)
    return y.reshape(B, T, D).astype(x.dtype)


def _encoder_layer(x, w_in, decay_fwd, decay_bwd, rpb, w_ret_out, w_na_out, w_mix_out, ln1_g, ln1_b,
                   wg, bg, we, be, w1, w3, w2, ln2_g, ln2_b):
    q_r, k_r, v_r, g_r, q_n, k_n, v_n, gate_r, gate_n = _split_in_proj(x, w_in)
    ret = _retention_branch(q_r, k_r, v_r, g_r, decay_fwd, decay_bwd) @ w_ret_out
    na = _neighbourhood_attention(q_n, k_n, v_n, rpb) @ w_na_out
    merged = jax.nn.sigmoid(gate_r) * ret + jax.nn.sigmoid(gate_n) * na
    x = _layer_norm(DEEPNORM_ALPHA * x + merged @ w_mix_out, ln1_g, ln1_b)
    x = _layer_norm(DEEPNORM_ALPHA * x + _hierarchical_moe(x, wg, bg, we, be, w1, w3, w2), ln2_g, ln2_b)
    return x


def setup_inputs(seed: int = 0) -> dict:
    key = jax.random.key(seed)
    ks = jax.random.split(key, 24)
    f32 = jnp.float32

    def nrm(k, shape, scale):
        return jax.random.normal(k, shape, f32) * scale

    x_prompt = nrm(ks[0], (BATCH, SEQ, D_MODEL), 1.0)
    x_sample = nrm(ks[1], (DEC_BATCH, DEC_SEQ, D_MODEL), 1.0)
    col_scale = jnp.concatenate([
        jnp.ones((2 * RET_QK_W,), f32),
        jnp.full((RET_V_W,), DEEPNORM_BETA, f32),
        jnp.ones((RET_V_W + 2 * NA_W,), f32),
        jnp.full((NA_W,), DEEPNORM_BETA, f32),
        jnp.ones((2 * D_MODEL,), f32)])
    w_in = nrm(ks[2], (DEPTH, D_MODEL, IN_COLS), D_MODEL ** -0.5) * col_scale
    base_logit = jnp.log(2.0 ** (5.0 + jnp.arange(RET_HEADS, dtype=f32)) - 1.0)
    ret_decay_fwd = base_logit + nrm(ks[3], (DEPTH, RET_HEADS), 0.1)
    ret_decay_bwd = base_logit + nrm(ks[4], (DEPTH, RET_HEADS), 0.1)
    na_rel_bias = nrm(ks[5], (DEPTH, NA_HEADS, 2 * NA_WIN_ROWS - 1, 2 * NA_WIN_COLS - 1), 0.1)
    w_ret_out = nrm(ks[6], (DEPTH, RET_V_W, D_MODEL), RET_V_W ** -0.5 * DEEPNORM_BETA)
    w_na_out = nrm(ks[7], (DEPTH, NA_W, D_MODEL), NA_W ** -0.5 * DEEPNORM_BETA)
    w_mix_out = nrm(ks[8], (DEPTH, D_MODEL, D_MODEL), D_MODEL ** -0.5 * DEEPNORM_BETA)
    ln1_g = 1.0 + nrm(ks[9], (DEPTH, D_MODEL), 0.02)
    ln1_b = nrm(ks[10], (DEPTH, D_MODEL), 0.02)
    router_group_w = nrm(ks[11], (DEPTH, D_MODEL, N_GROUPS), D_MODEL ** -0.5)
    router_group_b = nrm(ks[12], (DEPTH, N_GROUPS), 0.01)
    router_expert_w = nrm(ks[13], (DEPTH, D_MODEL, N_GROUPS, EXPERTS_PER_GROUP), D_MODEL ** -0.5)
    router_expert_b = nrm(ks[14], (DEPTH, N_GROUPS, EXPERTS_PER_GROUP), 0.01)
    expert_w1 = nrm(ks[15], (DEPTH, N_EXPERTS, D_MODEL, D_FF_EXPERT), D_MODEL ** -0.5)
    expert_w3 = nrm(ks[16], (DEPTH, N_EXPERTS, D_MODEL, D_FF_EXPERT), D_MODEL ** -0.5)
    expert_w2 = nrm(ks[17], (DEPTH, N_EXPERTS, D_FF_EXPERT, D_MODEL), D_FF_EXPERT ** -0.5 * DEEPNORM_BETA)
    ln2_g = 1.0 + nrm(ks[18], (DEPTH, D_MODEL), 0.02)
    ln2_b = nrm(ks[19], (DEPTH, D_MODEL), 0.02)
    return {'x_prompt': x_prompt, 'x_sample': x_sample, 'w_in': w_in,
            'ret_decay_fwd': ret_decay_fwd, 'ret_decay_bwd': ret_decay_bwd, 'na_rel_bias': na_rel_bias,
            'w_ret_out': w_ret_out, 'w_na_out': w_na_out, 'w_mix_out': w_mix_out,
            'ln1_g': ln1_g, 'ln1_b': ln1_b,
            'router_group_w': router_group_w, 'router_group_b': router_group_b,
            'router_expert_w': router_expert_w, 'router_expert_b': router_expert_b,
            'expert_w1': expert_w1, 'expert_w3': expert_w3, 'expert_w2': expert_w2,
            'ln2_g': ln2_g, 'ln2_b': ln2_b}


def reference(x_prompt, x_sample, w_in, ret_decay_fwd, ret_decay_bwd, na_rel_bias, w_ret_out, w_na_out,
              w_mix_out, ln1_g, ln1_b, router_group_w, router_group_b, router_expert_w, router_expert_b,
              expert_w1, expert_w3, expert_w2, ln2_g, ln2_b):
    def trunk(x):
        for l in range(DEPTH):
            x = _encoder_layer(x, w_in[l], ret_decay_fwd[l], ret_decay_bwd[l], na_rel_bias[l],
                               w_ret_out[l], w_na_out[l], w_mix_out[l], ln1_g[l], ln1_b[l],
                               router_group_w[l], router_group_b[l], router_expert_w[l], router_expert_b[l],
                               expert_w1[l], expert_w3[l], expert_w2[l], ln2_g[l], ln2_b[l])
        return x

    y_prompt = trunk(x_prompt)
    y_sample = trunk(x_sample)
    return (y_prompt, y_sample)
```

```python
import functools

import numpy as np
import jax
import jax.numpy as jnp
from jax import lax
from jax.experimental import pallas as pl
from jax.experimental.pallas import tpu as pltpu

F32 = jnp.float32
BF16 = jnp.bfloat16

D_MODEL = 1024
GRID_W = 64
RET_HEADS = 4
RET_DK = 256
RET_DV = 512
RET_QK_W = RET_HEADS * RET_DK
RET_V_W = RET_HEADS * RET_DV
ROPE_BASE = 10000.0
NA_HEADS = 16
NA_HEAD_DIM = 64
NA_W = NA_HEADS * NA_HEAD_DIM
NA_WIN_ROWS = 8
NA_WIN_COLS = 16
IN_COLS = 2 * RET_QK_W + 2 * RET_V_W + 3 * NA_W + 2 * D_MODEL
N_GROUPS = 4
EXPERTS_PER_GROUP = 4
N_EXPERTS = N_GROUPS * EXPERTS_PER_GROUP
D_FF_EXPERT = 256
LN_EPS = 1e-5
GN_EPS = 1e-5

COL_TILE = 1024
CB_QR, CB_KR, CB_VR, CB_GR, CB_QN, CB_KN, CB_VN, CB_GATE_R, CB_GATE_N = 0, 1, 2, 4, 6, 7, 8, 9, 10
N_COL_BLOCKS = IN_COLS // COL_TILE

ROUTER_LANES = 128
NA_ROWS_PER_STEP = 8
NA_HEADS_PER_GROUP = 4
NA_GROUP_W = NA_HEADS_PER_GROUP * NA_HEAD_DIM
NA_MASK_VALUE = -1e30
V7X_VMEM_BYTES = 64 * 1024 * 1024


def _vmem_limit(estimate_bytes):
    return int(min(V7X_VMEM_BYTES - 8 * 1024 * 1024, max(32 * 1024 * 1024, estimate_bytes * 3 // 2)))


def _sigmoid(x):
    return 1.0 / (1.0 + jnp.exp(-x))


def _layer_norm_rows(y, g, b):
    mu = jnp.mean(y, axis=-1, keepdims=True)
    d = y - mu
    var = jnp.mean(d * d, axis=-1, keepdims=True)
    return d * lax.rsqrt(var + LN_EPS) * g + b


def _in_proj_kernel(x_ref, w_ref, cos_ref, sin_ref, o_ref, xb_ref):
    j = pl.program_id(1)

    @pl.when(j == 0)
    def _():
        xb_ref[...] = x_ref[...].astype(BF16)

    acc = jnp.dot(xb_ref[...], w_ref[...], preferred_element_type=F32)

    @pl.when(j <= CB_KR)
    def _():
        scale = jnp.where(j == CB_KR, RET_DK ** -0.5, 1.0).astype(F32)
        cos = cos_ref[...] * scale
        sin = sin_ref[...] * scale
        half = RET_DK // 2
        for h in range(RET_HEADS):
            x1 = acc[:, h * RET_DK:h * RET_DK + half]
            x2 = acc[:, h * RET_DK + half:(h + 1) * RET_DK]
            o_ref[:, h * RET_DK:h * RET_DK + half] = (x1 * cos - x2 * sin).astype(BF16)
            o_ref[:, h * RET_DK + half:(h + 1) * RET_DK] = (x1 * sin + x2 * cos).astype(BF16)

    @pl.when(j > CB_KR)
    def _():
        scale = jnp.where(j == CB_QN, NA_HEAD_DIM ** -0.5, 1.0).astype(F32)
        o_ref[...] = (acc * scale).astype(BF16)


def _in_proj(x2, w_bf16, cos, sin, seq_len):
    n = x2.shape[0]
    tm = min(1024, seq_len)
    tiles_per_seq = seq_len // tm
    est = 2 * tm * D_MODEL * 4 + tm * D_MODEL * 2 + 4 * D_MODEL * COL_TILE * 2 // 2 * 2 + 2 * tm * COL_TILE * 2
    return pl.pallas_call(
        _in_proj_kernel,
        out_shape=jax.ShapeDtypeStruct((n, IN_COLS), BF16),
        grid=(n // tm, N_COL_BLOCKS),
        in_specs=[
            pl.BlockSpec((tm, D_MODEL), lambda i, j: (i, 0)),
            pl.BlockSpec((D_MODEL, COL_TILE), lambda i, j: (0, j)),
            pl.BlockSpec((tm, RET_DK // 2), lambda i, j: (i % tiles_per_seq, 0)),
            pl.BlockSpec((tm, RET_DK // 2), lambda i, j: (i % tiles_per_seq, 0)),
        ],
        out_specs=pl.BlockSpec((tm, COL_TILE), lambda i, j: (i, j)),
        scratch_shapes=[pltpu.VMEM((tm, D_MODEL), BF16)],
        compiler_params=pltpu.CompilerParams(
            dimension_semantics=("arbitrary", "arbitrary"), vmem_limit_bytes=_vmem_limit(est)),
        name="in_proj",
    )(x2, w_bf16, cos, sin)


def _ret_tables(lg_ref, chunk, dmat_ref, qd_ref, kd_ref, cd_ref, direction):
    ii = lax.broadcasted_iota(jnp.int32, (chunk, chunk), 0)
    jj = lax.broadcasted_iota(jnp.int32, (chunk, chunk), 1)
    diff = (ii - jj).astype(F32)
    pos_q = lax.broadcasted_iota(jnp.int32, (chunk, RET_DV), 0).astype(F32)
    pos_k = lax.broadcasted_iota(jnp.int32, (chunk, RET_DK), 0).astype(F32)
    for h in range(RET_HEADS):
        lf = lg_ref[0, h]
        lb = lg_ref[1, h]
        if dmat_ref is not None:
            dmat_ref[h] = jnp.where(diff >= 0, jnp.exp(jnp.maximum(diff, 0.0) * lf),
                                    jnp.exp(jnp.maximum(-diff, 0.0) * lb))
        if direction == 0:
            qd_ref[h] = jnp.exp((pos_q + 1.0) * lf)
            kd_ref[h] = jnp.exp((chunk - 1.0 - pos_k) * lf)
            cd_ref[h] = jnp.exp(jnp.zeros((8, RET_DV), F32) + chunk * lf)
        else:
            qd_ref[h] = jnp.exp((chunk - pos_q) * lb)
            kd_ref[h] = jnp.exp(pos_k * lb)
            cd_ref[h] = jnp.exp(jnp.zeros((8, RET_DV), F32) + chunk * lb)


def _ret_cross_and_update(q, k, v, h, state_ref, qd_ref, kd_ref, cd_ref):
    st = state_ref[h]
    cross = jnp.dot(q, st.astype(BF16), preferred_element_type=F32) * qd_ref[h]
    kd = (k.astype(F32) * kd_ref[h]).astype(BF16)
    upd = lax.dot_general(kd, v, (((0,), (0,)), ((), ())), preferred_element_type=F32)
    state_ref[h] = st * cd_ref[h, 0:1, :] + upd
    return cross


def _ret_fwd_kernel(lg_ref, q_ref, k_ref, v_ref, o_ref, state_ref, dmat_ref, qd_ref, kd_ref, cd_ref, *, chunk,
                    n_chunks):
    first = (pl.program_id(0) == 0) & (pl.program_id(1) == 0)

    @pl.when(first)
    def _():
        _ret_tables(lg_ref, chunk, dmat_ref, qd_ref, kd_ref, cd_ref, 0)

    @pl.when(pl.program_id(1) == 0)
    def _():
        state_ref[...] = jnp.zeros(state_ref.shape, F32)

    def body(c, carry):
        rows = pl.ds(pl.multiple_of(c * chunk, chunk), chunk)
        for h in range(RET_HEADS):
            q = q_ref[rows, h * RET_DK:(h + 1) * RET_DK]
            k = k_ref[rows, h * RET_DK:(h + 1) * RET_DK]
            v = v_ref[rows, h * RET_DV:(h + 1) * RET_DV]
            s = lax.dot_general(q, k, (((1,), (1,)), ((), ())), preferred_element_type=F32)
            p = (s * dmat_ref[h]).astype(BF16)
            o = jnp.dot(p, v, preferred_element_type=F32)
            o = o + _ret_cross_and_update(q, k, v, h, state_ref, qd_ref, kd_ref, cd_ref)
            o_ref[rows, h * RET_DV:(h + 1) * RET_DV] = o
        return carry

    lax.fori_loop(0, n_chunks, body, 0)


def _ret_bwd_kernel(lg_ref, q_ref, k_ref, v_ref, g_ref, oa_ref, w_ref, o_ref, state_ref, qd_ref, kd_ref, cd_ref,
                    gated_ref, *, chunk, n_chunks):
    first = (pl.program_id(0) == 0) & (pl.program_id(1) == 0)

    @pl.when(first)
    def _():
        _ret_tables(lg_ref, chunk, None, qd_ref, kd_ref, cd_ref, 1)

    @pl.when(pl.program_id(1) == 0)
    def _():
        state_ref[...] = jnp.zeros(state_ref.shape, F32)

    def body(cc, carry):
        c = n_chunks - 1 - cc
        rows = pl.ds(pl.multiple_of(c * chunk, chunk), chunk)
        for h in range(RET_HEADS):
            q = q_ref[rows, h * RET_DK:(h + 1) * RET_DK]
            k = k_ref[rows, h * RET_DK:(h + 1) * RET_DK]
            v = v_ref[rows, h * RET_DV:(h + 1) * RET_DV]
            o = oa_ref[rows, h * RET_DV:(h + 1) * RET_DV]
            o = o + _ret_cross_and_update(q, k, v, h, state_ref, qd_ref, kd_ref, cd_ref)
            mu = jnp.mean(o, axis=-1, keepdims=True)
            d = o - mu
            var = jnp.mean(d * d, axis=-1, keepdims=True)
            on = d * lax.rsqrt(var + GN_EPS)
            g = g_ref[rows, h * RET_DV:(h + 1) * RET_DV].astype(F32)
            gated_ref[rows, h * RET_DV:(h + 1) * RET_DV] = (g * _sigmoid(g) * on).astype(BF16)
        return carry

    lax.fori_loop(0, n_chunks, body, 0)
    o_ref[...] = jnp.dot(gated_ref[...], w_ref[...], preferred_element_type=F32)


def _ret_step_tokens(seq_len, chunk):
    return min(512, seq_len) // chunk * chunk


def _ret_fwd(proj, lg, batch, seq_len, chunk):
    ts = _ret_step_tokens(seq_len, chunk)
    nt = seq_len // ts
    n = batch * seq_len
    est = (2 * (2 * ts * RET_QK_W * 2 + ts * RET_V_W * 2 + ts * RET_V_W * 4)
           + RET_HEADS * (RET_DK * RET_DV * 4 + chunk * chunk * 4 + chunk * RET_DV * 4 + chunk * RET_DK * 4))
    kern = functools.partial(_ret_fwd_kernel, chunk=chunk, n_chunks=ts // chunk)
    return pl.pallas_call(
        kern,
        out_shape=jax.ShapeDtypeStruct((n, RET_V_W), F32),
        grid=(batch, nt),
        in_specs=[
            pl.BlockSpec(memory_space=pltpu.SMEM),
            pl.BlockSpec((ts, RET_QK_W), lambda b, t: (b * nt + t, CB_QR)),
            pl.BlockSpec((ts, RET_QK_W), lambda b, t: (b * nt + t, CB_KR)),
            pl.BlockSpec((ts, RET_V_W), lambda b, t: (b * nt + t, CB_VR * COL_TILE // RET_V_W)),
        ],
        out_specs=pl.BlockSpec((ts, RET_V_W), lambda b, t: (b * nt + t, 0)),
        scratch_shapes=[
            pltpu.VMEM((RET_HEADS, RET_DK, RET_DV), F32),
            pltpu.VMEM((RET_HEADS, chunk, chunk), F32),
            pltpu.VMEM((RET_HEADS, chunk, RET_DV), F32),
            pltpu.VMEM((RET_HEADS, chunk, RET_DK), F32),
            pltpu.VMEM((RET_HEADS, 8, RET_DV), F32),
        ],
        compiler_params=pltpu.CompilerParams(
            dimension_semantics=("arbitrary", "arbitrary"), vmem_limit_bytes=_vmem_limit(est)),
        name="ret_fwd",
    )(lg, proj, proj, proj)


def _ret_bwd(proj, oa, lg, w_ret_bf16, batch, seq_len, chunk):
    ts = _ret_step_tokens(seq_len, chunk)
    nt = seq_len // ts
    n = batch * seq_len
    est = (2 * (2 * ts * RET_QK_W * 2 + 2 * ts * RET_V_W * 2 + ts * RET_V_W * 4 + ts * D_MODEL * 4)
           + RET_V_W * D_MODEL * 2 + ts * RET_V_W * 2
           + RET_HEADS * (RET_DK * RET_DV * 4 + chunk * RET_DV * 4 + chunk * RET_DK * 4))
    kern = functools.partial(_ret_bwd_kernel, chunk=chunk, n_chunks=ts // chunk)

    def rev(b, t):
        return b * nt + (nt - 1 - t)

    return pl.pallas_call(
        kern,
        out_shape=jax.ShapeDtypeStruct((n, D_MODEL), F32),
        grid=(batch, nt),
        in_specs=[
            pl.BlockSpec(memory_space=pltpu.SMEM),
            pl.BlockSpec((ts, RET_QK_W), lambda b, t: (rev(b, t), CB_QR)),
            pl.BlockSpec((ts, RET_QK_W), lambda b, t: (rev(b, t), CB_KR)),
            pl.BlockSpec((ts, RET_V_W), lambda b, t: (rev(b, t), CB_VR * COL_TILE // RET_V_W)),
            pl.BlockSpec((ts, RET_V_W), lambda b, t: (rev(b, t), CB_GR * COL_TILE // RET_V_W)),
            pl.BlockSpec((ts, RET_V_W), lambda b, t: (rev(b, t), 0)),
            pl.BlockSpec(memory_space=pltpu.VMEM),
        ],
        out_specs=pl.BlockSpec((ts, D_MODEL), lambda b, t: (rev(b, t), 0)),
        scratch_shapes=[
            pltpu.VMEM((RET_HEADS, RET_DK, RET_DV), F32),
            pltpu.VMEM((RET_HEADS, chunk, RET_DV), F32),
            pltpu.VMEM((RET_HEADS, chunk, RET_DK), F32),
            pltpu.VMEM((RET_HEADS, 8, RET_DV), F32),
            pltpu.VMEM((ts, RET_V_W), BF16),
        ],
        compiler_params=pltpu.CompilerParams(
            dimension_semantics=("arbitrary", "arbitrary"), vmem_limit_bytes=_vmem_limit(est)),
        name="ret_bwd",
    )(lg, proj, proj, proj, proj, oa, w_ret_bf16)


def _na_bias_table(rpb):
    cols = np.arange(GRID_W)
    col_start = np.clip(cols - NA_WIN_COLS // 2, 0, GRID_W - NA_WIN_COLS)
    rel = cols[None, :] - cols[:, None] + NA_WIN_COLS - 1
    valid = (cols[None, :] >= col_start[:, None]) & (cols[None, :] < col_start[:, None] + NA_WIN_COLS)
    tb = rpb.astype(F32)[:, :, np.clip(rel, 0, 2 * NA_WIN_COLS - 2)]
    tb = jnp.where(valid[None, None], tb, NA_MASK_VALUE)
    var = jnp.stack([tb[:, s:s + NA_WIN_ROWS] for s in range(NA_WIN_ROWS)], 0)
    return var.transpose(0, 1, 3, 2, 4).reshape(NA_WIN_ROWS, NA_HEADS * GRID_W, NA_WIN_ROWS * GRID_W)


def _na_kernel(q_ref, kp_ref, kc_ref, kn_ref, vp_ref, vc_ref, vn_ref, bias_ref, o_ref, kbuf, vbuf, *, grid_rows):
    t = pl.program_id(1)
    blk = NA_ROWS_PER_STEP * GRID_W
    win = NA_WIN_ROWS * GRID_W
    kbuf[0:blk] = kp_ref[...]
    kbuf[blk:2 * blk] = kc_ref[...]
    kbuf[2 * blk:3 * blk] = kn_ref[...]
    vbuf[0:blk] = vp_ref[...]
    vbuf[blk:2 * blk] = vc_ref[...]
    vbuf[2 * blk:3 * blk] = vn_ref[...]

    rr = lax.broadcasted_iota(jnp.int32, (NA_GROUP_W, NA_GROUP_W), 0) // GRID_W
    ll = lax.broadcasted_iota(jnp.int32, (NA_GROUP_W, NA_GROUP_W), 1) // NA_HEAD_DIM
    head_mask = rr == ll

    def body(i, carry):
        r = t * NA_ROWS_PER_STEP + i
        r0 = jnp.clip(r - NA_WIN_ROWS // 2, 0, grid_rows - NA_WIN_ROWS)
        shift = r0 - r + NA_WIN_ROWS - 1
        off = pl.multiple_of((r0 - (t - 1) * NA_ROWS_PER_STEP) * GRID_W, GRID_W)
        qrows = pl.ds(pl.multiple_of(i * GRID_W, GRID_W), GRID_W)
        for g in range(NA_HEADS // NA_HEADS_PER_GROUP):
            lanes = slice(g * NA_GROUP_W, (g + 1) * NA_GROUP_W)
            q4 = q_ref[qrows, lanes]
            qm = jnp.where(head_mask, jnp.concatenate([q4] * NA_HEADS_PER_GROUP, axis=0), jnp.zeros((), BF16))
            k4 = kbuf[pl.ds(off, win), lanes]
            sc = lax.dot_general(qm, k4, (((1,), (1,)), ((), ())), preferred_element_type=F32)
            sc = sc + bias_ref[shift, lanes, :]
            m = jnp.max(sc, axis=-1, keepdims=True)
            p = jnp.exp(sc - m)
            l = jnp.sum(p, axis=-1, keepdims=True)
            v4 = vbuf[pl.ds(off, win), lanes]
            oall = jnp.dot(p.astype(BF16), v4, preferred_element_type=F32) * (1.0 / l)
            oall = jnp.where(head_mask, oall, 0.0)
            o4 = oall[0:GRID_W]
            for hh in range(1, NA_HEADS_PER_GROUP):
                o4 = o4 + oall[hh * GRID_W:(hh + 1) * GRID_W]
            o_ref[qrows, lanes] = o4.astype(o_ref.dtype)
        return carry

    lax.fori_loop(0, NA_ROWS_PER_STEP, body, 0)


def _na(proj, bias_tbl, batch, seq_len):
    grid_rows = seq_len // GRID_W
    assert grid_rows % NA_ROWS_PER_STEP == 0 and grid_rows >= NA_WIN_ROWS
    nb = grid_rows // NA_ROWS_PER_STEP
    blk = NA_ROWS_PER_STEP * GRID_W
    n = batch * seq_len
    est = (2 * 7 * blk * NA_W * 2 + 2 * blk * NA_W * 2 + 2 * 3 * blk * NA_W * 2
           + NA_WIN_ROWS * NA_W * NA_WIN_ROWS * GRID_W * 4)

    def cur(b, t):
        return b * nb + t

    def prev(b, t):
        return b * nb + jnp.maximum(t - 1, 0)

    def nxt(b, t):
        return b * nb + jnp.minimum(t + 1, nb - 1)

    kern = functools.partial(_na_kernel, grid_rows=grid_rows)
    return pl.pallas_call(
        kern,
        out_shape=jax.ShapeDtypeStruct((n, NA_W), BF16),
        grid=(batch, nb),
        in_specs=[
            pl.BlockSpec((blk, NA_W), lambda b, t: (cur(b, t), CB_QN)),
            pl.BlockSpec((blk, NA_W), lambda b, t: (prev(b, t), CB_KN)),
            pl.BlockSpec((blk, NA_W), lambda b, t: (cur(b, t), CB_KN)),
            pl.BlockSpec((blk, NA_W), lambda b, t: (nxt(b, t), CB_KN)),
            pl.BlockSpec((blk, NA_W), lambda b, t: (prev(b, t), CB_VN)),
            pl.BlockSpec((blk, NA_W), lambda b, t: (cur(b, t), CB_VN)),
            pl.BlockSpec((blk, NA_W), lambda b, t: (nxt(b, t), CB_VN)),
            pl.BlockSpec(memory_space=pltpu.VMEM),
        ],
        out_specs=pl.BlockSpec((blk, NA_W), lambda b, t: (cur(b, t), 0)),
        scratch_shapes=[pltpu.VMEM((3 * blk, NA_W), BF16), pltpu.VMEM((3 * blk, NA_W), BF16)],
        compiler_params=pltpu.CompilerParams(
            dimension_semantics=("arbitrary", "arbitrary"), vmem_limit_bytes=_vmem_limit(est)),
        name="na",
    )(proj, proj, proj, proj, proj, proj, proj, bias_tbl)


def _post_kernel(x_ref, ret_ref, na_ref, gr_ref, gn_ref, wna_ref, wmix_ref, g_ref, b_ref, wr_ref, br_ref, x1_ref,
                 lg_ref, *, alpha):
    na_p = jnp.dot(na_ref[...], wna_ref[...], preferred_element_type=F32)
    merged = (_sigmoid(gr_ref[...].astype(F32)) * ret_ref[...] + _sigmoid(gn_ref[...].astype(F32)) * na_p)
    y = alpha * x_ref[...] + jnp.dot(merged.astype(BF16), wmix_ref[...], preferred_element_type=F32)
    x1 = _layer_norm_rows(y, g_ref[...], b_ref[...])
    x1_ref[...] = x1
    lg_ref[...] = jnp.dot(x1.astype(BF16), wr_ref[...], preferred_element_type=F32) + br_ref[...]


def _post(x2, ret, na, proj, wna, wmix, ln_g, ln_b, wr, br, alpha):
    n = x2.shape[0]
    tm = min(512, n)
    est = (2 * tm * D_MODEL * (4 + 4 + 2 + 2 + 2 + 4) + 2 * tm * ROUTER_LANES * 4
           + 2 * D_MODEL * D_MODEL * 2 + D_MODEL * ROUTER_LANES * 2)
    row = lambda i: (i, 0)
    whole = pl.BlockSpec(memory_space=pltpu.VMEM)
    return pl.pallas_call(
        functools.partial(_post_kernel, alpha=alpha),
        out_shape=(jax.ShapeDtypeStruct((n, D_MODEL), F32), jax.ShapeDtypeStruct((n, ROUTER_LANES), F32)),
        grid=(n // tm,),
        in_specs=[
            pl.BlockSpec((tm, D_MODEL), row),
            pl.BlockSpec((tm, D_MODEL), row),
            pl.BlockSpec((tm, NA_W), row),
            pl.BlockSpec((tm, D_MODEL), lambda i: (i, CB_GATE_R)),
            pl.BlockSpec((tm, D_MODEL), lambda i: (i, CB_GATE_N)),
            whole, whole, whole, whole, whole, whole,
        ],
        out_specs=(pl.BlockSpec((tm, D_MODEL), row), pl.BlockSpec((tm, ROUTER_LANES), row)),
        compiler_params=pltpu.CompilerParams(
            dimension_semantics=("arbitrary",), vmem_limit_bytes=_vmem_limit(est)),
        name="post",
    )(x2, ret, na, proj, proj, wna, wmix, ln_g, ln_b, wr, br)


def _moe_gates(logits):
    gl = [logits[:, i:i + 1] for i in range(N_GROUPS)]
    gmax = functools.reduce(jnp.maximum, gl)
    denom = functools.reduce(lambda a, b: a + b, [jnp.exp(v - gmax) for v in gl])
    group_p = 1.0 / denom
    gidx = jnp.full(gmax.shape, N_GROUPS - 1, jnp.int32)
    for i in range(N_GROUPS - 2, -1, -1):
        gidx = jnp.where(gl[i] == gmax, i, gidx)
    el = []
    for e in range(EXPERTS_PER_GROUP):
        v = logits[:, N_GROUPS + (N_GROUPS - 1) * EXPERTS_PER_GROUP + e:N_GROUPS + (N_GROUPS - 1) * EXPERTS_PER_GROUP + e + 1]
        for gi in range(N_GROUPS - 2, -1, -1):
            c = N_GROUPS + gi * EXPERTS_PER_GROUP + e
            v = jnp.where(gidx == gi, logits[:, c:c + 1], v)
        el.append(v)
    m1 = functools.reduce(jnp.maximum, el)
    i1 = jnp.full(m1.shape, EXPERTS_PER_GROUP - 1, jnp.int32)
    for e in range(EXPERTS_PER_GROUP - 2, -1, -1):
        i1 = jnp.where(el[e] == m1, e, i1)
    neg = jnp.float32(-jnp.inf)
    rest = [jnp.where(i1 == e, neg, el[e]) for e in range(EXPERTS_PER_GROUP)]
    m2 = functools.reduce(jnp.maximum, rest)
    i2 = jnp.full(m2.shape, EXPERTS_PER_GROUP - 1, jnp.int32)
    for e in range(EXPERTS_PER_GROUP - 2, -1, -1):
        i2 = jnp.where((rest[e] == m2) & (i1 != e), e, i2)
    e2 = jnp.exp(m2 - m1)
    w1 = group_p / (1.0 + e2)
    w2 = group_p * e2 / (1.0 + e2)
    gates = []
    for gi in range(N_GROUPS):
        for e in range(EXPERTS_PER_GROUP):
            in_g = gidx == gi
            gates.append(jnp.where(in_g & (i1 == e), w1, 0.0) + jnp.where(in_g & (i2 == e), w2, 0.0))
    return gates


def _moe_kernel(x1_ref, lg_ref, w13_ref, w2_ref, g_ref, b_ref, o_ref, h_ref, *, alpha):
    x1 = x1_ref[...]
    xb = x1.astype(BF16)
    gates = _moe_gates(lg_ref[...])
    for e in range(N_EXPERTS):
        hh = jnp.dot(xb, w13_ref[e], preferred_element_type=F32)
        a = hh[:, :D_FF_EXPERT]
        h = a * _sigmoid(a) * hh[:, D_FF_EXPERT:] * gates[e]
        h_ref[:, e * D_FF_EXPERT:(e + 1) * D_FF_EXPERT] = h.astype(BF16)
    y = alpha * x1 + jnp.dot(h_ref[...], w2_ref[...], preferred_element_type=F32)
    o_ref[...] = _layer_norm_rows(y, g_ref[...], b_ref[...])


def _moe(x1, logits, w13, w2, ln_g, ln_b, alpha):
    n = x1.shape[0]
    tm = min(512, n)
    est = (2 * tm * D_MODEL * 8 + 2 * tm * ROUTER_LANES * 4 + N_EXPERTS * D_MODEL * 2 * D_FF_EXPERT * 2
           + N_EXPERTS * D_FF_EXPERT * D_MODEL * 2 + tm * N_EXPERTS * D_FF_EXPERT * 2)
    row = lambda i: (i, 0)
    whole = pl.BlockSpec(memory_space=pltpu.VMEM)
    return pl.pallas_call(
        functools.partial(_moe_kernel, alpha=alpha),
        out_shape=jax.ShapeDtypeStruct((n, D_MODEL), F32),
        grid=(n // tm,),
        in_specs=[pl.BlockSpec((tm, D_MODEL), row), pl.BlockSpec((tm, ROUTER_LANES), row), whole, whole, whole, whole],
        out_specs=pl.BlockSpec((tm, D_MODEL), row),
        scratch_shapes=[pltpu.VMEM((tm, N_EXPERTS * D_FF_EXPERT), BF16)],
        compiler_params=pltpu.CompilerParams(
            dimension_semantics=("arbitrary",), vmem_limit_bytes=_vmem_limit(est)),
        name="moe",
    )(x1, logits, w13, w2, ln_g, ln_b)


def _rotary_tables(seq_len):
    half = RET_DK // 2
    inv_freq = ROPE_BASE ** (-jnp.arange(half, dtype=F32) / half)
    ang = jnp.arange(seq_len, dtype=F32)[:, None] * inv_freq[None, :]
    return jnp.cos(ang), jnp.sin(ang)


def _prepare_layer(w_in, decay_fwd, decay_bwd, rpb, w_ret_out, w_na_out, w_mix_out, ln1_g, ln1_b, wg, bg, we, be, w1,
                   w3, w2, ln2_g, ln2_b):
    pad = ROUTER_LANES - N_GROUPS - N_EXPERTS
    wr = jnp.concatenate([wg, we.reshape(D_MODEL, N_EXPERTS), jnp.zeros((D_MODEL, pad), F32)], axis=1)
    br = jnp.concatenate([bg.astype(F32), be.reshape(N_EXPERTS).astype(F32), jnp.zeros((pad,), F32)])
    return dict(
        w_in=w_in.astype(BF16),
        lg=jnp.stack([jax.nn.log_sigmoid(decay_fwd.astype(F32)), jax.nn.log_sigmoid(decay_bwd.astype(F32))]),
        bias_tbl=_na_bias_table(rpb),
        w_ret=w_ret_out.astype(BF16), w_na=w_na_out.astype(BF16), w_mix=w_mix_out.astype(BF16),
        ln1_g=ln1_g.astype(F32).reshape(1, D_MODEL), ln1_b=ln1_b.astype(F32).reshape(1, D_MODEL),
        wr=wr.astype(BF16), br=br.reshape(1, ROUTER_LANES),
        w13=jnp.concatenate([w1, w3], axis=-1).astype(BF16),
        w2=w2.reshape(N_EXPERTS * D_FF_EXPERT, D_MODEL).astype(BF16),
        ln2_g=ln2_g.astype(F32).reshape(1, D_MODEL), ln2_b=ln2_b.astype(F32).reshape(1, D_MODEL),
    )


def _encoder_layer(x2, p, batch, seq_len, cos, sin, alpha, chunk):
    proj = _in_proj(x2, p["w_in"], cos, sin, seq_len)
    oa = _ret_fwd(proj, p["lg"], batch, seq_len, chunk)
    ret = _ret_bwd(proj, oa, p["lg"], p["w_ret"], batch, seq_len, chunk)
    na = _na(proj, p["bias_tbl"], batch, seq_len)
    x1, logits = _post(x2, ret, na, proj, p["w_na"], p["w_mix"], p["ln1_g"], p["ln1_b"], p["wr"], p["br"], alpha)
    return _moe(x1, logits, p["w13"], p["w2"], p["ln2_g"], p["ln2_b"], alpha)


def _trunk(x, layers, alpha, chunk=128):
    batch, seq_len, _ = x.shape
    cos, sin = _rotary_tables(seq_len)
    x2 = x.reshape(batch * seq_len, D_MODEL)
    for p in layers:
        x2 = _encoder_layer(x2, p, batch, seq_len, cos, sin, alpha, chunk)
    return x2.reshape(batch, seq_len, D_MODEL)


def kernel(x_prompt, x_sample, w_in, ret_decay_fwd, ret_decay_bwd, na_rel_bias, w_ret_out, w_na_out, w_mix_out, ln1_g,
           ln1_b, router_group_w, router_group_b, router_expert_w, router_expert_b, expert_w1, expert_w3, expert_w2,
           ln2_g, ln2_b):
    depth = w_in.shape[0]
    alpha = (2 * depth) ** 0.25
    per_layer = (w_in, ret_decay_fwd, ret_decay_bwd, na_rel_bias, w_ret_out, w_na_out, w_mix_out, ln1_g, ln1_b,
                 router_group_w, router_group_b, router_expert_w, router_expert_b, expert_w1, expert_w3, expert_w2,
                 ln2_g, ln2_b)
    layers = [_prepare_layer(*[a[l] for a in per_layer]) for l in range(depth)]
    return (_trunk(x_prompt, layers, alpha), _trunk(x_sample, layers, alpha))
```

```python
import functools

import numpy as np
import jax
import jax.numpy as jnp
from jax import lax
from jax.experimental import pallas as pl
from jax.experimental.pallas import tpu as pltpu

F32 = jnp.float32
BF16 = jnp.bfloat16

D_MODEL = 1024
GRID_W = 64
RET_HEADS = 4
RET_DK = 256
RET_DV = 512
RET_QK_W = RET_HEADS * RET_DK
RET_V_W = RET_HEADS * RET_DV
ROPE_BASE = 10000.0
RET_CHUNK = 256
NA_HEADS = 16
NA_HEAD_DIM = 64
NA_W = NA_HEADS * NA_HEAD_DIM
NA_WIN_ROWS = 8
NA_WIN_COLS = 16
IN_COLS = 2 * RET_QK_W + 2 * RET_V_W + 3 * NA_W + 2 * D_MODEL
N_GROUPS = 4
EXPERTS_PER_GROUP = 4
N_EXPERTS = N_GROUPS * EXPERTS_PER_GROUP
D_FF_EXPERT = 256
LN_EPS = 1e-5
GN_EPS = 1e-5

COL_TILE = 1024
QK_COLS = 2 * RET_QK_W
REST_COLS = IN_COLS - QK_COLS
CB_QR, CB_KR = 0, 1
CB_VR, CB_GR, CB_QN, CB_KN, CB_VN, CB_GATE_R, CB_GATE_N = 0, 2, 4, 5, 6, 7, 8

ROUTER_LANES = 128
POST_SUB_ROWS = 256
NA_ROWS_PER_STEP = 8
NA_HEADS_PER_GROUP = 4
NA_GROUP_W = NA_HEADS_PER_GROUP * NA_HEAD_DIM
NA_MASK_VALUE = -1e30
V7X_VMEM_BYTES = 64 * 1024 * 1024


def _vmem_limit(estimate_bytes):
    return int(min(V7X_VMEM_BYTES - 8 * 1024 * 1024, max(32 * 1024 * 1024, estimate_bytes * 3 // 2)))


def _sigmoid(x):
    return 1.0 / (1.0 + jnp.exp(-x))


def _layer_norm_rows(y, g, b):
    mu = jnp.mean(y, axis=-1, keepdims=True)
    d = y - mu
    var = jnp.mean(d * d, axis=-1, keepdims=True)
    return d * lax.rsqrt(var + LN_EPS) * g + b


def _in_proj_rotary_kernel(x_ref, w_ref, cos_ref, sin_ref, o_ref, xb_ref):
    @pl.when(pl.program_id(1) == 0)
    def _():
        xb_ref[...] = x_ref[...].astype(BF16)

    acc = jnp.dot(xb_ref[...], w_ref[...], preferred_element_type=F32)
    cos = cos_ref[...]
    sin = sin_ref[...]
    half = RET_DK // 2
    for h in range(RET_HEADS):
        x1 = acc[:, h * RET_DK:h * RET_DK + half]
        x2 = acc[:, h * RET_DK + half:(h + 1) * RET_DK]
        o_ref[:, h * RET_DK:h * RET_DK + half] = (x1 * cos - x2 * sin).astype(BF16)
        o_ref[:, h * RET_DK + half:(h + 1) * RET_DK] = (x1 * sin + x2 * cos).astype(BF16)


def _in_proj_plain_kernel(x_ref, w_ref, o_ref, xb_ref):
    @pl.when(pl.program_id(1) == 0)
    def _():
        xb_ref[...] = x_ref[...].astype(BF16)

    o_ref[...] = jnp.dot(xb_ref[...], w_ref[...], preferred_element_type=F32).astype(BF16)


def _in_proj(x2, w_qk, w_rest, cos, sin, seq_len):
    n = x2.shape[0]
    tm = min(1024, seq_len)
    tiles_per_seq = seq_len // tm
    est = (2 * tm * D_MODEL * 4 + tm * D_MODEL * 2 + 2 * D_MODEL * COL_TILE * 2 + 2 * tm * COL_TILE * 2
           + tm * COL_TILE * 4)
    params = pltpu.CompilerParams(dimension_semantics=("arbitrary", "arbitrary"), vmem_limit_bytes=_vmem_limit(est))
    x_spec = pl.BlockSpec((tm, D_MODEL), lambda i, j: (i, 0))
    w_spec = pl.BlockSpec((D_MODEL, COL_TILE), lambda i, j: (0, j))
    o_spec = pl.BlockSpec((tm, COL_TILE), lambda i, j: (i, j))
    rot_spec = pl.BlockSpec((tm, RET_DK // 2), lambda i, j: (i % tiles_per_seq, 0))
    qk = pl.pallas_call(
        _in_proj_rotary_kernel,
        out_shape=jax.ShapeDtypeStruct((n, QK_COLS), BF16),
        grid=(n // tm, QK_COLS // COL_TILE),
        in_specs=[x_spec, w_spec, rot_spec, rot_spec],
        out_specs=o_spec,
        scratch_shapes=[pltpu.VMEM((tm, D_MODEL), BF16)],
        compiler_params=params,
        name="in_proj_qk",
    )(x2, w_qk, cos, sin)
    rest = pl.pallas_call(
        _in_proj_plain_kernel,
        out_shape=jax.ShapeDtypeStruct((n, REST_COLS), BF16),
        grid=(n // tm, REST_COLS // COL_TILE),
        in_specs=[x_spec, w_spec],
        out_specs=o_spec,
        scratch_shapes=[pltpu.VMEM((tm, D_MODEL), BF16)],
        compiler_params=params,
        name="in_proj_rest",
    )(x2, w_rest)
    return qk, rest


def _ret_tables(lg_ref, chunk, dmat_ref, qd_ref, kd_ref, cd_ref, direction):
    ii = lax.broadcasted_iota(jnp.int32, (chunk, chunk), 0)
    jj = lax.broadcasted_iota(jnp.int32, (chunk, chunk), 1)
    diff = (ii - jj).astype(F32)
    pos_q = lax.broadcasted_iota(jnp.int32, (chunk, RET_DV), 0).astype(F32)
    pos_k = lax.broadcasted_iota(jnp.int32, (chunk, RET_DK), 0).astype(F32)
    for h in range(RET_HEADS):
        lf = lg_ref[0, h]
        lb = lg_ref[1, h]
        if dmat_ref is not None:
            dmat_ref[h] = jnp.where(diff >= 0, jnp.exp(jnp.maximum(diff, 0.0) * lf),
                                    jnp.exp(jnp.maximum(-diff, 0.0) * lb))
        if direction == 0:
            qd_ref[h] = jnp.exp((pos_q + 1.0) * lf)
            kd_ref[h] = jnp.exp((chunk - 1.0 - pos_k) * lf)
            cd_ref[h] = jnp.exp(jnp.zeros((8, RET_DV), F32) + chunk * lf)
        else:
            qd_ref[h] = jnp.exp((chunk - pos_q) * lb)
            kd_ref[h] = jnp.exp(pos_k * lb)
            cd_ref[h] = jnp.exp(jnp.zeros((8, RET_DV), F32) + chunk * lb)


def _ret_cross_and_update(q, k, v, h, state_ref, qd_ref, kd_ref, cd_ref):
    st = state_ref[h]
    cross = jnp.dot(q, st.astype(BF16), preferred_element_type=F32) * qd_ref[h]
    kd = (k.astype(F32) * kd_ref[h]).astype(BF16)
    upd = lax.dot_general(kd, v, (((0,), (0,)), ((), ())), preferred_element_type=F32)
    state_ref[h] = st * cd_ref[h, 0:1, :] + upd
    return cross


def _ret_fwd_kernel(lg_ref, q_ref, k_ref, v_ref, o_ref, state_ref, dmat_ref, qd_ref, kd_ref, cd_ref, *, chunk,
                    n_chunks):
    first = (pl.program_id(0) == 0) & (pl.program_id(1) == 0)

    @pl.when(first)
    def _():
        _ret_tables(lg_ref, chunk, dmat_ref, qd_ref, kd_ref, cd_ref, 0)

    @pl.when(pl.program_id(1) == 0)
    def _():
        state_ref[...] = jnp.zeros(state_ref.shape, F32)

    def body(c, carry):
        rows = pl.ds(pl.multiple_of(c * chunk, chunk), chunk)
        for h in range(RET_HEADS):
            q = q_ref[rows, h * RET_DK:(h + 1) * RET_DK]
            k = k_ref[rows, h * RET_DK:(h + 1) * RET_DK]
            v = v_ref[rows, h * RET_DV:(h + 1) * RET_DV]
            s = lax.dot_general(q, k, (((1,), (1,)), ((), ())), preferred_element_type=F32)
            p = (s * dmat_ref[h]).astype(BF16)
            o = jnp.dot(p, v, preferred_element_type=F32)
            o = o + _ret_cross_and_update(q, k, v, h, state_ref, qd_ref, kd_ref, cd_ref)
            o_ref[rows, h * RET_DV:(h + 1) * RET_DV] = o.astype(o_ref.dtype)
        return carry

    lax.fori_loop(0, n_chunks, body, 0)


def _ret_bwd_kernel(lg_ref, q_ref, k_ref, v_ref, oa_ref, o_ref, state_ref, qd_ref, kd_ref, cd_ref, *, chunk, n_chunks):
    first = (pl.program_id(0) == 0) & (pl.program_id(1) == 0)

    @pl.when(first)
    def _():
        _ret_tables(lg_ref, chunk, None, qd_ref, kd_ref, cd_ref, 1)

    @pl.when(pl.program_id(1) == 0)
    def _():
        state_ref[...] = jnp.zeros(state_ref.shape, F32)

    def body(cc, carry):
        c = n_chunks - 1 - cc
        rows = pl.ds(pl.multiple_of(c * chunk, chunk), chunk)
        for h in range(RET_HEADS):
            q = q_ref[rows, h * RET_DK:(h + 1) * RET_DK]
            k = k_ref[rows, h * RET_DK:(h + 1) * RET_DK]
            v = v_ref[rows, h * RET_DV:(h + 1) * RET_DV]
            o = oa_ref[rows, h * RET_DV:(h + 1) * RET_DV].astype(F32)
            o = o + _ret_cross_and_update(q, k, v, h, state_ref, qd_ref, kd_ref, cd_ref)
            o_ref[rows, h * RET_DV:(h + 1) * RET_DV] = o.astype(o_ref.dtype)
        return carry

    lax.fori_loop(0, n_chunks, body, 0)


def _ret_step_tokens(seq_len, chunk):
    return min(512, seq_len) // chunk * chunk


def _ret_fwd(qk, rest, lg, batch, seq_len, chunk):
    ts = _ret_step_tokens(seq_len, chunk)
    nt = seq_len // ts
    n = batch * seq_len
    est = (2 * (2 * ts * RET_QK_W * 2 + ts * RET_V_W * 2 + ts * RET_V_W * 2)
           + RET_HEADS * (RET_DK * RET_DV * 4 + chunk * chunk * 4 + chunk * RET_DV * 4 + chunk * RET_DK * 4))
    kern = functools.partial(_ret_fwd_kernel, chunk=chunk, n_chunks=ts // chunk)
    return pl.pallas_call(
        kern,
        out_shape=jax.ShapeDtypeStruct((n, RET_V_W), BF16),
        grid=(batch, nt),
        in_specs=[
            pl.BlockSpec(memory_space=pltpu.SMEM),
            pl.BlockSpec((ts, RET_QK_W), lambda b, t: (b * nt + t, CB_QR)),
            pl.BlockSpec((ts, RET_QK_W), lambda b, t: (b * nt + t, CB_KR)),
            pl.BlockSpec((ts, RET_V_W), lambda b, t: (b * nt + t, CB_VR * COL_TILE // RET_V_W)),
        ],
        out_specs=pl.BlockSpec((ts, RET_V_W), lambda b, t: (b * nt + t, 0)),
        scratch_shapes=[
            pltpu.VMEM((RET_HEADS, RET_DK, RET_DV), F32),
            pltpu.VMEM((RET_HEADS, chunk, chunk), F32),
            pltpu.VMEM((RET_HEADS, chunk, RET_DV), F32),
            pltpu.VMEM((RET_HEADS, chunk, RET_DK), F32),
            pltpu.VMEM((RET_HEADS, 8, RET_DV), F32),
        ],
        compiler_params=pltpu.CompilerParams(
            dimension_semantics=("arbitrary", "arbitrary"), vmem_limit_bytes=_vmem_limit(est)),
        name="ret_fwd",
    )(lg, qk, qk, rest)


def _ret_bwd(qk, rest, oa, lg, batch, seq_len, chunk):
    ts = _ret_step_tokens(seq_len, chunk)
    nt = seq_len // ts
    n = batch * seq_len
    est = (2 * (2 * ts * RET_QK_W * 2 + 3 * ts * RET_V_W * 2)
           + RET_HEADS * (RET_DK * RET_DV * 4 + chunk * RET_DV * 4 + chunk * RET_DK * 4))
    kern = functools.partial(_ret_bwd_kernel, chunk=chunk, n_chunks=ts // chunk)

    def rev(b, t):
        return b * nt + (nt - 1 - t)

    return pl.pallas_call(
        kern,
        out_shape=jax.ShapeDtypeStruct((n, RET_V_W), BF16),
        grid=(batch, nt),
        in_specs=[
            pl.BlockSpec(memory_space=pltpu.SMEM),
            pl.BlockSpec((ts, RET_QK_W), lambda b, t: (rev(b, t), CB_QR)),
            pl.BlockSpec((ts, RET_QK_W), lambda b, t: (rev(b, t), CB_KR)),
            pl.BlockSpec((ts, RET_V_W), lambda b, t: (rev(b, t), CB_VR * COL_TILE // RET_V_W)),
            pl.BlockSpec((ts, RET_V_W), lambda b, t: (rev(b, t), 0)),
        ],
        out_specs=pl.BlockSpec((ts, RET_V_W), lambda b, t: (rev(b, t), 0)),
        scratch_shapes=[
            pltpu.VMEM((RET_HEADS, RET_DK, RET_DV), F32),
            pltpu.VMEM((RET_HEADS, chunk, RET_DV), F32),
            pltpu.VMEM((RET_HEADS, chunk, RET_DK), F32),
            pltpu.VMEM((RET_HEADS, 8, RET_DV), F32),
        ],
        compiler_params=pltpu.CompilerParams(
            dimension_semantics=("arbitrary", "arbitrary"), vmem_limit_bytes=_vmem_limit(est)),
        name="ret_bwd",
    )(lg, qk, qk, rest, oa)


def _na_bias_table(rpb):
    cols = np.arange(GRID_W)
    col_start = np.clip(cols - NA_WIN_COLS // 2, 0, GRID_W - NA_WIN_COLS)
    rel = cols[None, :] - cols[:, None] + NA_WIN_COLS - 1
    valid = (cols[None, :] >= col_start[:, None]) & (cols[None, :] < col_start[:, None] + NA_WIN_COLS)
    tb = rpb.astype(F32)[:, :, np.clip(rel, 0, 2 * NA_WIN_COLS - 2)]
    tb = jnp.where(valid[None, None], tb, NA_MASK_VALUE)
    var = jnp.stack([tb[:, s:s + NA_WIN_ROWS] for s in range(NA_WIN_ROWS)], 0)
    return var.transpose(0, 1, 3, 2, 4).reshape(NA_WIN_ROWS, NA_HEADS * GRID_W, NA_WIN_ROWS * GRID_W)


def _na_kernel(q_ref, kp_ref, kc_ref, kn_ref, vp_ref, vc_ref, vn_ref, bias_ref, o_ref, kbuf, vbuf, *, grid_rows):
    t = pl.program_id(1)
    blk = NA_ROWS_PER_STEP * GRID_W
    win = NA_WIN_ROWS * GRID_W
    kbuf[0:blk] = kp_ref[...]
    kbuf[blk:2 * blk] = kc_ref[...]
    kbuf[2 * blk:3 * blk] = kn_ref[...]
    vbuf[0:blk] = vp_ref[...]
    vbuf[blk:2 * blk] = vc_ref[...]
    vbuf[2 * blk:3 * blk] = vn_ref[...]

    rr = lax.broadcasted_iota(jnp.int32, (NA_GROUP_W, NA_GROUP_W), 0) // GRID_W
    ll = lax.broadcasted_iota(jnp.int32, (NA_GROUP_W, NA_GROUP_W), 1) // NA_HEAD_DIM
    head_mask = rr == ll

    def body(i, carry):
        r = t * NA_ROWS_PER_STEP + i
        r0 = jnp.clip(r - NA_WIN_ROWS // 2, 0, grid_rows - NA_WIN_ROWS)
        shift = r0 - r + NA_WIN_ROWS - 1
        off = pl.multiple_of((r0 - (t - 1) * NA_ROWS_PER_STEP) * GRID_W, GRID_W)
        qrows = pl.ds(pl.multiple_of(i * GRID_W, GRID_W), GRID_W)
        for g in range(NA_HEADS // NA_HEADS_PER_GROUP):
            lanes = slice(g * NA_GROUP_W, (g + 1) * NA_GROUP_W)
            q4 = q_ref[qrows, lanes]
            qm = jnp.where(head_mask, jnp.concatenate([q4] * NA_HEADS_PER_GROUP, axis=0), jnp.zeros((), BF16))
            k4 = kbuf[pl.ds(off, win), lanes]
            sc = lax.dot_general(qm, k4, (((1,), (1,)), ((), ())), preferred_element_type=F32)
            sc = sc + bias_ref[shift, lanes, :]
            m = jnp.max(sc, axis=-1, keepdims=True)
            p = jnp.exp(sc - m)
            l = jnp.sum(p, axis=-1, keepdims=True)
            v4 = vbuf[pl.ds(off, win), lanes]
            oall = jnp.dot(p.astype(BF16), v4, preferred_element_type=F32) * (1.0 / l)
            oall = jnp.where(head_mask, oall, 0.0)
            o4 = oall[0:GRID_W]
            for hh in range(1, NA_HEADS_PER_GROUP):
                o4 = o4 + oall[hh * GRID_W:(hh + 1) * GRID_W]
            o_ref[qrows, lanes] = o4.astype(o_ref.dtype)
        return carry

    lax.fori_loop(0, NA_ROWS_PER_STEP, body, 0)


def _na(rest, bias_tbl, batch, seq_len):
    grid_rows = seq_len // GRID_W
    assert grid_rows % NA_ROWS_PER_STEP == 0 and grid_rows >= NA_WIN_ROWS
    nb = grid_rows // NA_ROWS_PER_STEP
    blk = NA_ROWS_PER_STEP * GRID_W
    n = batch * seq_len
    est = (2 * 7 * blk * NA_W * 2 + 2 * blk * NA_W * 2 + 2 * 3 * blk * NA_W * 2
           + NA_WIN_ROWS * NA_W * NA_WIN_ROWS * GRID_W * 4)

    def cur(b, t):
        return b * nb + t

    def prev(b, t):
        return b * nb + jnp.maximum(t - 1, 0)

    def nxt(b, t):
        return b * nb + jnp.minimum(t + 1, nb - 1)

    kern = functools.partial(_na_kernel, grid_rows=grid_rows)
    return pl.pallas_call(
        kern,
        out_shape=jax.ShapeDtypeStruct((n, NA_W), BF16),
        grid=(batch, nb),
        in_specs=[
            pl.BlockSpec((blk, NA_W), lambda b, t: (cur(b, t), CB_QN)),
            pl.BlockSpec((blk, NA_W), lambda b, t: (prev(b, t), CB_KN)),
            pl.BlockSpec((blk, NA_W), lambda b, t: (cur(b, t), CB_KN)),
            pl.BlockSpec((blk, NA_W), lambda b, t: (nxt(b, t), CB_KN)),
            pl.BlockSpec((blk, NA_W), lambda b, t: (prev(b, t), CB_VN)),
            pl.BlockSpec((blk, NA_W), lambda b, t: (cur(b, t), CB_VN)),
            pl.BlockSpec((blk, NA_W), lambda b, t: (nxt(b, t), CB_VN)),
            pl.BlockSpec(memory_space=pltpu.VMEM),
        ],
        out_specs=pl.BlockSpec((blk, NA_W), lambda b, t: (cur(b, t), 0)),
        scratch_shapes=[pltpu.VMEM((3 * blk, NA_W), BF16), pltpu.VMEM((3 * blk, NA_W), BF16)],
        compiler_params=pltpu.CompilerParams(
            dimension_semantics=("arbitrary", "arbitrary"), vmem_limit_bytes=_vmem_limit(est)),
        name="na",
    )(rest, rest, rest, rest, rest, rest, rest, bias_tbl)


def _post_kernel(x_ref, o_ref, og_ref, na_ref, gr_ref, gn_ref, wret_ref, wna_ref, wmix_ref, g_ref, b_ref, wr_ref,
                 br_ref, x1_ref, lg_ref, *, alpha, sub_rows):
    for r0 in range(0, x_ref.shape[0], sub_rows):
        rows = slice(r0, r0 + sub_rows)
        gated = []
        for h in range(RET_HEADS):
            lanes = slice(h * RET_DV, (h + 1) * RET_DV)
            o = o_ref[rows, lanes].astype(F32)
            mu = jnp.mean(o, axis=-1, keepdims=True)
            d = o - mu
            var = jnp.mean(d * d, axis=-1, keepdims=True)
            og = og_ref[rows, lanes].astype(F32)
            gated.append((og * _sigmoid(og) * (d * lax.rsqrt(var + GN_EPS))).astype(BF16))
        ret = jnp.dot(jnp.concatenate(gated, axis=1), wret_ref[...], preferred_element_type=F32)
        na_p = jnp.dot(na_ref[rows, :], wna_ref[...], preferred_element_type=F32)
        merged = _sigmoid(gr_ref[rows, :].astype(F32)) * ret + _sigmoid(gn_ref[rows, :].astype(F32)) * na_p
        y = alpha * x_ref[rows, :] + jnp.dot(merged.astype(BF16), wmix_ref[...], preferred_element_type=F32)
        x1 = _layer_norm_rows(y, g_ref[...], b_ref[...])
        x1_ref[rows, :] = x1
        lg_ref[rows, :] = jnp.dot(x1.astype(BF16), wr_ref[...], preferred_element_type=F32) + br_ref[...]


def _post(x2, o_ret, na, rest, wret, wna, wmix, ln_g, ln_b, wr, br, alpha):
    n = x2.shape[0]
    tm = min(512, n)
    est = (2 * tm * D_MODEL * (4 + 2 + 2 + 2 + 4) + 2 * 2 * tm * RET_V_W * 2 + 2 * tm * ROUTER_LANES * 4
           + tm * RET_V_W * 2 + (RET_V_W + 2 * D_MODEL + ROUTER_LANES) * D_MODEL * 2)
    row = lambda i: (i, 0)
    whole = pl.BlockSpec(memory_space=pltpu.VMEM)
    return pl.pallas_call(
        functools.partial(_post_kernel, alpha=alpha, sub_rows=min(POST_SUB_ROWS, tm)),
        out_shape=(jax.ShapeDtypeStruct((n, D_MODEL), F32), jax.ShapeDtypeStruct((n, ROUTER_LANES), F32)),
        grid=(n // tm,),
        in_specs=[
            pl.BlockSpec((tm, D_MODEL), row),
            pl.BlockSpec((tm, RET_V_W), row),
            pl.BlockSpec((tm, RET_V_W), lambda i: (i, CB_GR * COL_TILE // RET_V_W)),
            pl.BlockSpec((tm, NA_W), row),
            pl.BlockSpec((tm, D_MODEL), lambda i: (i, CB_GATE_R)),
            pl.BlockSpec((tm, D_MODEL), lambda i: (i, CB_GATE_N)),
            whole, whole, whole, whole, whole, whole, whole,
        ],
        out_specs=(pl.BlockSpec((tm, D_MODEL), row), pl.BlockSpec((tm, ROUTER_LANES), row)),
        compiler_params=pltpu.CompilerParams(
            dimension_semantics=("arbitrary",), vmem_limit_bytes=_vmem_limit(est)),
        name="post",
    )(x2, o_ret, rest, na, rest, rest, wret, wna, wmix, ln_g, ln_b, wr, br)


def _moe_gates(logits):
    gl = [logits[:, i:i + 1] for i in range(N_GROUPS)]
    gmax = functools.reduce(jnp.maximum, gl)
    denom = functools.reduce(lambda a, b: a + b, [jnp.exp(v - gmax) for v in gl])
    group_p = 1.0 / denom
    gidx = jnp.full(gmax.shape, N_GROUPS - 1, jnp.int32)
    for i in range(N_GROUPS - 2, -1, -1):
        gidx = jnp.where(gl[i] == gmax, i, gidx)
    el = []
    for e in range(EXPERTS_PER_GROUP):
        v = logits[:, N_GROUPS + (N_GROUPS - 1) * EXPERTS_PER_GROUP + e:N_GROUPS + (N_GROUPS - 1) * EXPERTS_PER_GROUP + e + 1]
        for gi in range(N_GROUPS - 2, -1, -1):
            c = N_GROUPS + gi * EXPERTS_PER_GROUP + e
            v = jnp.where(gidx == gi, logits[:, c:c + 1], v)
        el.append(v)
    m1 = functools.reduce(jnp.maximum, el)
    i1 = jnp.full(m1.shape, EXPERTS_PER_GROUP - 1, jnp.int32)
    for e in range(EXPERTS_PER_GROUP - 2, -1, -1):
        i1 = jnp.where(el[e] == m1, e, i1)
    neg = jnp.float32(-jnp.inf)
    rest = [jnp.where(i1 == e, neg, el[e]) for e in range(EXPERTS_PER_GROUP)]
    m2 = functools.reduce(jnp.maximum, rest)
    i2 = jnp.full(m2.shape, EXPERTS_PER_GROUP - 1, jnp.int32)
    for e in range(EXPERTS_PER_GROUP - 2, -1, -1):
        i2 = jnp.where((rest[e] == m2) & (i1 != e), e, i2)
    e2 = jnp.exp(m2 - m1)
    w1 = group_p / (1.0 + e2)
    w2 = group_p * e2 / (1.0 + e2)
    gates = []
    for gi in range(N_GROUPS):
        for e in range(EXPERTS_PER_GROUP):
            in_g = gidx == gi
            gates.append(jnp.where(in_g & (i1 == e), w1, 0.0) + jnp.where(in_g & (i2 == e), w2, 0.0))
    return gates


def _moe_kernel(x1_ref, lg_ref, w13_ref, w2_ref, g_ref, b_ref, o_ref, h_ref, *, alpha):
    x1 = x1_ref[...]
    xb = x1.astype(BF16)
    gates = _moe_gates(lg_ref[...])
    for e in range(N_EXPERTS):
        hh = jnp.dot(xb, w13_ref[e], preferred_element_type=F32)
        a = hh[:, :D_FF_EXPERT]
        h = a * _sigmoid(a) * hh[:, D_FF_EXPERT:] * gates[e]
        h_ref[:, e * D_FF_EXPERT:(e + 1) * D_FF_EXPERT] = h.astype(BF16)
    y = alpha * x1 + jnp.dot(h_ref[...], w2_ref[...], preferred_element_type=F32)
    o_ref[...] = _layer_norm_rows(y, g_ref[...], b_ref[...])


def _moe(x1, logits, w13, w2, ln_g, ln_b, alpha):
    n = x1.shape[0]
    tm = min(512, n)
    est = (2 * tm * D_MODEL * 8 + 2 * tm * ROUTER_LANES * 4 + N_EXPERTS * D_MODEL * 2 * D_FF_EXPERT * 2
           + N_EXPERTS * D_FF_EXPERT * D_MODEL * 2 + tm * N_EXPERTS * D_FF_EXPERT * 2)
    row = lambda i: (i, 0)
    whole = pl.BlockSpec(memory_space=pltpu.VMEM)
    return pl.pallas_call(
        functools.partial(_moe_kernel, alpha=alpha),
        out_shape=jax.ShapeDtypeStruct((n, D_MODEL), F32),
        grid=(n // tm,),
        in_specs=[pl.BlockSpec((tm, D_MODEL), row), pl.BlockSpec((tm, ROUTER_LANES), row), whole, whole, whole, whole],
        out_specs=pl.BlockSpec((tm, D_MODEL), row),
        scratch_shapes=[pltpu.VMEM((tm, N_EXPERTS * D_FF_EXPERT), BF16)],
        compiler_params=pltpu.CompilerParams(
            dimension_semantics=("arbitrary",), vmem_limit_bytes=_vmem_limit(est)),
        name="moe",
    )(x1, logits, w13, w2, ln_g, ln_b)


def _rotary_tables(seq_len):
    half = RET_DK // 2
    inv_freq = ROPE_BASE ** (-jnp.arange(half, dtype=F32) / half)
    ang = jnp.arange(seq_len, dtype=F32)[:, None] * inv_freq[None, :]
    return jnp.cos(ang), jnp.sin(ang)


def _prepare_layer(w_in, decay_fwd, decay_bwd, rpb, w_ret_out, w_na_out, w_mix_out, ln1_g, ln1_b, wg, bg, we, be, w1,
                   w3, w2, ln2_g, ln2_b):
    pad = ROUTER_LANES - N_GROUPS - N_EXPERTS
    wr = jnp.concatenate([wg, we.reshape(D_MODEL, N_EXPERTS), jnp.zeros((D_MODEL, pad), F32)], axis=1)
    br = jnp.concatenate([bg.astype(F32), be.reshape(N_EXPERTS).astype(F32), jnp.zeros((pad,), F32)])
    col_scale = jnp.ones((IN_COLS,), F32)
    col_scale = col_scale.at[RET_QK_W:2 * RET_QK_W].set(RET_DK ** -0.5)
    qn0 = QK_COLS + CB_QN * COL_TILE
    col_scale = col_scale.at[qn0:qn0 + NA_W].set(NA_HEAD_DIM ** -0.5)
    w_in = (w_in.astype(F32) * col_scale).astype(BF16)
    return dict(
        w_qk=w_in[:, :QK_COLS], w_rest=w_in[:, QK_COLS:],
        lg=jnp.stack([jax.nn.log_sigmoid(decay_fwd.astype(F32)), jax.nn.log_sigmoid(decay_bwd.astype(F32))]),
        bias_tbl=_na_bias_table(rpb),
        w_ret=w_ret_out.astype(BF16), w_na=w_na_out.astype(BF16), w_mix=w_mix_out.astype(BF16),
        ln1_g=ln1_g.astype(F32).reshape(1, D_MODEL), ln1_b=ln1_b.astype(F32).reshape(1, D_MODEL),
        wr=wr.astype(BF16), br=br.reshape(1, ROUTER_LANES),
        w13=jnp.concatenate([w1, w3], axis=-1).astype(BF16),
        w2=w2.reshape(N_EXPERTS * D_FF_EXPERT, D_MODEL).astype(BF16),
        ln2_g=ln2_g.astype(F32).reshape(1, D_MODEL), ln2_b=ln2_b.astype(F32).reshape(1, D_MODEL),
    )


def _encoder_layer(x2, p, batch, seq_len, cos, sin, alpha, chunk):
    qk, rest = _in_proj(x2, p["w_qk"], p["w_rest"], cos, sin, seq_len)
    oa = _ret_fwd(qk, rest, p["lg"], batch, seq_len, chunk)
    o_ret = _ret_bwd(qk, rest, oa, p["lg"], batch, seq_len, chunk)
    na = _na(rest, p["bias_tbl"], batch, seq_len)
    x1, logits = _post(x2, o_ret, na, rest, p["w_ret"], p["w_na"], p["w_mix"], p["ln1_g"], p["ln1_b"], p["wr"],
                       p["br"], alpha)
    return _moe(x1, logits, p["w13"], p["w2"], p["ln2_g"], p["ln2_b"], alpha)


def _trunk(x, layers, alpha, chunk=RET_CHUNK):
    batch, seq_len, _ = x.shape
    cos, sin = _rotary_tables(seq_len)
    x2 = x.reshape(batch * seq_len, D_MODEL)
    for p in layers:
        x2 = _encoder_layer(x2, p, batch, seq_len, cos, sin, alpha, chunk)
    return x2.reshape(batch, seq_len, D_MODEL)


def kernel(x_prompt, x_sample, w_in, ret_decay_fwd, ret_decay_bwd, na_rel_bias, w_ret_out, w_na_out, w_mix_out, ln1_g,
           ln1_b, router_group_w, router_group_b, router_expert_w, router_expert_b, expert_w1, expert_w3, expert_w2,
           ln2_g, ln2_b):
    depth = w_in.shape[0]
    alpha = (2 * depth) ** 0.25
    per_layer = (w_in, ret_decay_fwd, ret_decay_bwd, na_rel_bias, w_ret_out, w_na_out, w_mix_out, ln1_g, ln1_b,
                 router_group_w, router_group_b, router_expert_w, router_expert_b, expert_w1, expert_w3, expert_w2,
                 ln2_g, ln2_b)
    layers = [_prepare_layer(*[a[l] for a in per_layer]) for l in range(depth)]
    return (_trunk(x_prompt, layers, alpha), _trunk(x_sample, layers, alpha))
```

```python
import functools

import numpy as np
import jax
import jax.numpy as jnp
from jax import lax
from jax.experimental import pallas as pl
from jax.experimental.pallas import tpu as pltpu

F32 = jnp.float32
BF16 = jnp.bfloat16

D_MODEL = 1024
GRID_W = 64
RET_HEADS = 4
RET_DK = 256
RET_DV = 512
RET_QK_W = RET_HEADS * RET_DK
RET_V_W = RET_HEADS * RET_DV
ROPE_BASE = 10000.0
RET_CHUNK = 256
NA_HEADS = 16
NA_HEAD_DIM = 64
NA_W = NA_HEADS * NA_HEAD_DIM
NA_WIN_ROWS = 8
NA_WIN_COLS = 16
IN_COLS = 2 * RET_QK_W + 2 * RET_V_W + 3 * NA_W + 2 * D_MODEL
N_GROUPS = 4
EXPERTS_PER_GROUP = 4
N_EXPERTS = N_GROUPS * EXPERTS_PER_GROUP
D_FF_EXPERT = 256
LN_EPS = 1e-5
GN_EPS = 1e-5

COL_TILE = 1024
QK_COLS = 2 * RET_QK_W
REST_COLS = IN_COLS - QK_COLS
CB_QR, CB_KR = 0, 1
CB_VR, CB_GR, CB_QN, CB_KN, CB_VN, CB_GATE_R, CB_GATE_N = 0, 2, 4, 5, 6, 7, 8

ROUTER_LANES = 128
POST_SUB_ROWS = 256
MOE_ROW_W = D_MODEL + ROUTER_LANES
MOE_TILE = 256
NA_ROWS_PER_STEP = 8
NA_HEADS_PER_GROUP = 4
NA_GROUP_W = NA_HEADS_PER_GROUP * NA_HEAD_DIM
NA_MASK_VALUE = -1e30
V7X_VMEM_BYTES = 64 * 1024 * 1024


def _vmem_limit(estimate_bytes):
    return int(min(V7X_VMEM_BYTES - 8 * 1024 * 1024, max(32 * 1024 * 1024, estimate_bytes * 3 // 2)))


def _sigmoid(x):
    return 1.0 / (1.0 + jnp.exp(-x))


def _layer_norm_rows(y, g, b):
    mu = jnp.mean(y, axis=-1, keepdims=True)
    d = y - mu
    var = jnp.mean(d * d, axis=-1, keepdims=True)
    return d * lax.rsqrt(var + LN_EPS) * g + b


def _in_proj_rotary_kernel(x_ref, w_ref, cos_ref, sin_ref, o_ref, xb_ref):
    @pl.when(pl.program_id(1) == 0)
    def _():
        xb_ref[...] = x_ref[...].astype(BF16)

    acc = jnp.dot(xb_ref[...], w_ref[...], preferred_element_type=F32)
    cos = cos_ref[...]
    sin = sin_ref[...]
    half = RET_DK // 2
    for h in range(RET_HEADS):
        x1 = acc[:, h * RET_DK:h * RET_DK + half]
        x2 = acc[:, h * RET_DK + half:(h + 1) * RET_DK]
        o_ref[:, h * RET_DK:h * RET_DK + half] = (x1 * cos - x2 * sin).astype(BF16)
        o_ref[:, h * RET_DK + half:(h + 1) * RET_DK] = (x1 * sin + x2 * cos).astype(BF16)


def _in_proj_plain_kernel(x_ref, w_ref, o_ref, xb_ref):
    @pl.when(pl.program_id(1) == 0)
    def _():
        xb_ref[...] = x_ref[...].astype(BF16)

    o_ref[...] = jnp.dot(xb_ref[...], w_ref[...], preferred_element_type=F32).astype(BF16)


def _in_proj(x2, w_qk, w_rest, cos, sin, seq_len):
    n = x2.shape[0]
    tm = min(1024, seq_len)
    tiles_per_seq = seq_len // tm
    est = (2 * tm * D_MODEL * 4 + tm * D_MODEL * 2 + 2 * D_MODEL * COL_TILE * 2 + 2 * tm * COL_TILE * 2
           + tm * COL_TILE * 4)
    params = pltpu.CompilerParams(dimension_semantics=("arbitrary", "arbitrary"), vmem_limit_bytes=_vmem_limit(est))
    x_spec = pl.BlockSpec((tm, D_MODEL), lambda i, j: (i, 0))
    w_spec = pl.BlockSpec((D_MODEL, COL_TILE), lambda i, j: (0, j))
    o_spec = pl.BlockSpec((tm, COL_TILE), lambda i, j: (i, j))
    rot_spec = pl.BlockSpec((tm, RET_DK // 2), lambda i, j: (i % tiles_per_seq, 0))
    qk = pl.pallas_call(
        _in_proj_rotary_kernel,
        out_shape=jax.ShapeDtypeStruct((n, QK_COLS), BF16),
        grid=(n // tm, QK_COLS // COL_TILE),
        in_specs=[x_spec, w_spec, rot_spec, rot_spec],
        out_specs=o_spec,
        scratch_shapes=[pltpu.VMEM((tm, D_MODEL), BF16)],
        compiler_params=params,
        name="in_proj_qk",
    )(x2, w_qk, cos, sin)
    rest = pl.pallas_call(
        _in_proj_plain_kernel,
        out_shape=jax.ShapeDtypeStruct((n, REST_COLS), BF16),
        grid=(n // tm, REST_COLS // COL_TILE),
        in_specs=[x_spec, w_spec],
        out_specs=o_spec,
        scratch_shapes=[pltpu.VMEM((tm, D_MODEL), BF16)],
        compiler_params=params,
        name="in_proj_rest",
    )(x2, w_rest)
    return qk, rest


def _ret_tables(lg_ref, chunk, dmat_ref, qd_ref, kd_ref, cd_ref, direction):
    ii = lax.broadcasted_iota(jnp.int32, (chunk, chunk), 0)
    jj = lax.broadcasted_iota(jnp.int32, (chunk, chunk), 1)
    diff = (ii - jj).astype(F32)
    pos_q = lax.broadcasted_iota(jnp.int32, (chunk, RET_DV), 0).astype(F32)
    pos_k = lax.broadcasted_iota(jnp.int32, (chunk, RET_DK), 0).astype(F32)
    for h in range(RET_HEADS):
        lf = lg_ref[0, h]
        lb = lg_ref[1, h]
        if dmat_ref is not None:
            dmat_ref[h] = jnp.where(diff >= 0, jnp.exp(jnp.maximum(diff, 0.0) * lf),
                                    jnp.exp(jnp.maximum(-diff, 0.0) * lb))
        if direction == 0:
            qd_ref[h] = jnp.exp((pos_q + 1.0) * lf)
            kd_ref[h] = jnp.exp((chunk - 1.0 - pos_k) * lf)
            cd_ref[h] = jnp.exp(jnp.zeros((8, RET_DV), F32) + chunk * lf)
        else:
            qd_ref[h] = jnp.exp((chunk - pos_q) * lb)
            kd_ref[h] = jnp.exp(pos_k * lb)
            cd_ref[h] = jnp.exp(jnp.zeros((8, RET_DV), F32) + chunk * lb)


def _ret_cross_and_update(q, k, v, h, state_ref, qd_ref, kd_ref, cd_ref):
    st = state_ref[h]
    cross = jnp.dot(q, st.astype(BF16), preferred_element_type=F32) * qd_ref[h]
    kd = (k.astype(F32) * kd_ref[h]).astype(BF16)
    upd = lax.dot_general(kd, v, (((0,), (0,)), ((), ())), preferred_element_type=F32)
    state_ref[h] = st * cd_ref[h, 0:1, :] + upd
    return cross


def _ret_fwd_kernel(lg_ref, q_ref, k_ref, v_ref, o_ref, state_ref, dmat_ref, qd_ref, kd_ref, cd_ref, *, chunk,
                    n_chunks):
    first = (pl.program_id(0) == 0) & (pl.program_id(1) == 0)

    @pl.when(first)
    def _():
        _ret_tables(lg_ref, chunk, dmat_ref, qd_ref, kd_ref, cd_ref, 0)

    @pl.when(pl.program_id(1) == 0)
    def _():
        state_ref[...] = jnp.zeros(state_ref.shape, F32)

    def body(c, carry):
        rows = pl.ds(pl.multiple_of(c * chunk, chunk), chunk)
        for h in range(RET_HEADS):
            q = q_ref[rows, h * RET_DK:(h + 1) * RET_DK]
            k = k_ref[rows, h * RET_DK:(h + 1) * RET_DK]
            v = v_ref[rows, h * RET_DV:(h + 1) * RET_DV]
            s = lax.dot_general(q, k, (((1,), (1,)), ((), ())), preferred_element_type=F32)
            p = (s * dmat_ref[h]).astype(BF16)
            o = jnp.dot(p, v, preferred_element_type=F32)
            o = o + _ret_cross_and_update(q, k, v, h, state_ref, qd_ref, kd_ref, cd_ref)
            o_ref[rows, h * RET_DV:(h + 1) * RET_DV] = o.astype(o_ref.dtype)
        return carry

    lax.fori_loop(0, n_chunks, body, 0)


def _ret_bwd_kernel(lg_ref, q_ref, k_ref, v_ref, oa_ref, o_ref, state_ref, qd_ref, kd_ref, cd_ref, *, chunk, n_chunks):
    first = (pl.program_id(0) == 0) & (pl.program_id(1) == 0)

    @pl.when(first)
    def _():
        _ret_tables(lg_ref, chunk, None, qd_ref, kd_ref, cd_ref, 1)

    @pl.when(pl.program_id(1) == 0)
    def _():
        state_ref[...] = jnp.zeros(state_ref.shape, F32)

    def body(cc, carry):
        c = n_chunks - 1 - cc
        rows = pl.ds(pl.multiple_of(c * chunk, chunk), chunk)
        for h in range(RET_HEADS):
            q = q_ref[rows, h * RET_DK:(h + 1) * RET_DK]
            k = k_ref[rows, h * RET_DK:(h + 1) * RET_DK]
            v = v_ref[rows, h * RET_DV:(h + 1) * RET_DV]
            o = oa_ref[rows, h * RET_DV:(h + 1) * RET_DV].astype(F32)
            o = o + _ret_cross_and_update(q, k, v, h, state_ref, qd_ref, kd_ref, cd_ref)
            o_ref[rows, h * RET_DV:(h + 1) * RET_DV] = o.astype(o_ref.dtype)
        return carry

    lax.fori_loop(0, n_chunks, body, 0)


def _ret_step_tokens(seq_len, chunk):
    return min(512, seq_len) // chunk * chunk


def _ret_fwd(qk, rest, lg, batch, seq_len, chunk):
    ts = _ret_step_tokens(seq_len, chunk)
    nt = seq_len // ts
    n = batch * seq_len
    est = (2 * (2 * ts * RET_QK_W * 2 + ts * RET_V_W * 2 + ts * RET_V_W * 2)
           + RET_HEADS * (RET_DK * RET_DV * 4 + chunk * chunk * 4 + chunk * RET_DV * 4 + chunk * RET_DK * 4))
    kern = functools.partial(_ret_fwd_kernel, chunk=chunk, n_chunks=ts // chunk)
    return pl.pallas_call(
        kern,
        out_shape=jax.ShapeDtypeStruct((n, RET_V_W), BF16),
        grid=(batch, nt),
        in_specs=[
            pl.BlockSpec(memory_space=pltpu.SMEM),
            pl.BlockSpec((ts, RET_QK_W), lambda b, t: (b * nt + t, CB_QR)),
            pl.BlockSpec((ts, RET_QK_W), lambda b, t: (b * nt + t, CB_KR)),
            pl.BlockSpec((ts, RET_V_W), lambda b, t: (b * nt + t, CB_VR * COL_TILE // RET_V_W)),
        ],
        out_specs=pl.BlockSpec((ts, RET_V_W), lambda b, t: (b * nt + t, 0)),
        scratch_shapes=[
            pltpu.VMEM((RET_HEADS, RET_DK, RET_DV), F32),
            pltpu.VMEM((RET_HEADS, chunk, chunk), F32),
            pltpu.VMEM((RET_HEADS, chunk, RET_DV), F32),
            pltpu.VMEM((RET_HEADS, chunk, RET_DK), F32),
            pltpu.VMEM((RET_HEADS, 8, RET_DV), F32),
        ],
        compiler_params=pltpu.CompilerParams(
            dimension_semantics=("arbitrary", "arbitrary"), vmem_limit_bytes=_vmem_limit(est)),
        name="ret_fwd",
    )(lg, qk, qk, rest)


def _ret_bwd(qk, rest, oa, lg, batch, seq_len, chunk):
    ts = _ret_step_tokens(seq_len, chunk)
    nt = seq_len // ts
    n = batch * seq_len
    est = (2 * (2 * ts * RET_QK_W * 2 + 3 * ts * RET_V_W * 2)
           + RET_HEADS * (RET_DK * RET_DV * 4 + chunk * RET_DV * 4 + chunk * RET_DK * 4))
    kern = functools.partial(_ret_bwd_kernel, chunk=chunk, n_chunks=ts // chunk)

    def rev(b, t):
        return b * nt + (nt - 1 - t)

    return pl.pallas_call(
        kern,
        out_shape=jax.ShapeDtypeStruct((n, RET_V_W), BF16),
        grid=(batch, nt),
        in_specs=[
            pl.BlockSpec(memory_space=pltpu.SMEM),
            pl.BlockSpec((ts, RET_QK_W), lambda b, t: (rev(b, t), CB_QR)),
            pl.BlockSpec((ts, RET_QK_W), lambda b, t: (rev(b, t), CB_KR)),
            pl.BlockSpec((ts, RET_V_W), lambda b, t: (rev(b, t), CB_VR * COL_TILE // RET_V_W)),
            pl.BlockSpec((ts, RET_V_W), lambda b, t: (rev(b, t), 0)),
        ],
        out_specs=pl.BlockSpec((ts, RET_V_W), lambda b, t: (rev(b, t), 0)),
        scratch_shapes=[
            pltpu.VMEM((RET_HEADS, RET_DK, RET_DV), F32),
            pltpu.VMEM((RET_HEADS, chunk, RET_DV), F32),
            pltpu.VMEM((RET_HEADS, chunk, RET_DK), F32),
            pltpu.VMEM((RET_HEADS, 8, RET_DV), F32),
        ],
        compiler_params=pltpu.CompilerParams(
            dimension_semantics=("arbitrary", "arbitrary"), vmem_limit_bytes=_vmem_limit(est)),
        name="ret_bwd",
    )(lg, qk, qk, rest, oa)


def _na_bias_table(rpb):
    cols = np.arange(GRID_W)
    col_start = np.clip(cols - NA_WIN_COLS // 2, 0, GRID_W - NA_WIN_COLS)
    rel = cols[None, :] - cols[:, None] + NA_WIN_COLS - 1
    valid = (cols[None, :] >= col_start[:, None]) & (cols[None, :] < col_start[:, None] + NA_WIN_COLS)
    tb = rpb.astype(F32)[:, :, np.clip(rel, 0, 2 * NA_WIN_COLS - 2)]
    tb = jnp.where(valid[None, None], tb, NA_MASK_VALUE)
    var = jnp.stack([tb[:, s:s + NA_WIN_ROWS] for s in range(NA_WIN_ROWS)], 0)
    return var.transpose(0, 1, 3, 2, 4).reshape(NA_WIN_ROWS, NA_HEADS * GRID_W, NA_WIN_ROWS * GRID_W)


def _na_kernel(q_ref, kp_ref, kc_ref, kn_ref, vp_ref, vc_ref, vn_ref, bias_ref, o_ref, kbuf, vbuf, *, grid_rows):
    t = pl.program_id(1)
    blk = NA_ROWS_PER_STEP * GRID_W
    win = NA_WIN_ROWS * GRID_W
    kbuf[0:blk] = kp_ref[...]
    kbuf[blk:2 * blk] = kc_ref[...]
    kbuf[2 * blk:3 * blk] = kn_ref[...]
    vbuf[0:blk] = vp_ref[...]
    vbuf[blk:2 * blk] = vc_ref[...]
    vbuf[2 * blk:3 * blk] = vn_ref[...]

    rr = lax.broadcasted_iota(jnp.int32, (NA_GROUP_W, NA_GROUP_W), 0) // GRID_W
    ll = lax.broadcasted_iota(jnp.int32, (NA_GROUP_W, NA_GROUP_W), 1) // NA_HEAD_DIM
    head_mask = rr == ll

    def body(i, carry):
        r = t * NA_ROWS_PER_STEP + i
        r0 = jnp.clip(r - NA_WIN_ROWS // 2, 0, grid_rows - NA_WIN_ROWS)
        shift = r0 - r + NA_WIN_ROWS - 1
        off = pl.multiple_of((r0 - (t - 1) * NA_ROWS_PER_STEP) * GRID_W, GRID_W)
        qrows = pl.ds(pl.multiple_of(i * GRID_W, GRID_W), GRID_W)
        for g in range(NA_HEADS // NA_HEADS_PER_GROUP):
            lanes = slice(g * NA_GROUP_W, (g + 1) * NA_GROUP_W)
            q4 = q_ref[qrows, lanes]
            qm = jnp.where(head_mask, jnp.concatenate([q4] * NA_HEADS_PER_GROUP, axis=0), jnp.zeros((), BF16))
            k4 = kbuf[pl.ds(off, win), lanes]
            sc = lax.dot_general(qm, k4, (((1,), (1,)), ((), ())), preferred_element_type=F32)
            sc = sc + bias_ref[shift, lanes, :]
            m = jnp.max(sc, axis=-1, keepdims=True)
            p = jnp.exp(sc - m)
            l = jnp.sum(p, axis=-1, keepdims=True)
            v4 = vbuf[pl.ds(off, win), lanes]
            oall = jnp.dot(p.astype(BF16), v4, preferred_element_type=F32) * (1.0 / l)
            oall = jnp.where(head_mask, oall, 0.0)
            o4 = oall[0:GRID_W]
            for hh in range(1, NA_HEADS_PER_GROUP):
                o4 = o4 + oall[hh * GRID_W:(hh + 1) * GRID_W]
            o_ref[qrows, lanes] = o4.astype(o_ref.dtype)
        return carry

    lax.fori_loop(0, NA_ROWS_PER_STEP, body, 0)


def _na(rest, bias_tbl, batch, seq_len):
    grid_rows = seq_len // GRID_W
    assert grid_rows % NA_ROWS_PER_STEP == 0 and grid_rows >= NA_WIN_ROWS
    nb = grid_rows // NA_ROWS_PER_STEP
    blk = NA_ROWS_PER_STEP * GRID_W
    n = batch * seq_len
    est = (2 * 7 * blk * NA_W * 2 + 2 * blk * NA_W * 2 + 2 * 3 * blk * NA_W * 2
           + NA_WIN_ROWS * NA_W * NA_WIN_ROWS * GRID_W * 4)

    def cur(b, t):
        return b * nb + t

    def prev(b, t):
        return b * nb + jnp.maximum(t - 1, 0)

    def nxt(b, t):
        return b * nb + jnp.minimum(t + 1, nb - 1)

    kern = functools.partial(_na_kernel, grid_rows=grid_rows)
    return pl.pallas_call(
        kern,
        out_shape=jax.ShapeDtypeStruct((n, NA_W), BF16),
        grid=(batch, nb),
        in_specs=[
            pl.BlockSpec((blk, NA_W), lambda b, t: (cur(b, t), CB_QN)),
            pl.BlockSpec((blk, NA_W), lambda b, t: (prev(b, t), CB_KN)),
            pl.BlockSpec((blk, NA_W), lambda b, t: (cur(b, t), CB_KN)),
            pl.BlockSpec((blk, NA_W), lambda b, t: (nxt(b, t), CB_KN)),
            pl.BlockSpec((blk, NA_W), lambda b, t: (prev(b, t), CB_VN)),
            pl.BlockSpec((blk, NA_W), lambda b, t: (cur(b, t), CB_VN)),
            pl.BlockSpec((blk, NA_W), lambda b, t: (nxt(b, t), CB_VN)),
            pl.BlockSpec(memory_space=pltpu.VMEM),
        ],
        out_specs=pl.BlockSpec((blk, NA_W), lambda b, t: (cur(b, t), 0)),
        scratch_shapes=[pltpu.VMEM((3 * blk, NA_W), BF16), pltpu.VMEM((3 * blk, NA_W), BF16)],
        compiler_params=pltpu.CompilerParams(
            dimension_semantics=("arbitrary", "arbitrary"), vmem_limit_bytes=_vmem_limit(est)),
        name="na",
    )(rest, rest, rest, rest, rest, rest, rest, bias_tbl)


def _post_kernel(x_ref, o_ref, og_ref, na_ref, gr_ref, gn_ref, wret_ref, wna_ref, wmix_ref, g_ref, b_ref, wr_ref,
                 br_ref, x1e_ref, *, alpha, sub_rows):
    for r0 in range(0, x_ref.shape[0], sub_rows):
        rows = slice(r0, r0 + sub_rows)
        gated = []
        for h in range(RET_HEADS):
            lanes = slice(h * RET_DV, (h + 1) * RET_DV)
            o = o_ref[rows, lanes].astype(F32)
            mu = jnp.mean(o, axis=-1, keepdims=True)
            d = o - mu
            var = jnp.mean(d * d, axis=-1, keepdims=True)
            og = og_ref[rows, lanes].astype(F32)
            gated.append((og * _sigmoid(og) * (d * lax.rsqrt(var + GN_EPS))).astype(BF16))
        ret = jnp.dot(jnp.concatenate(gated, axis=1), wret_ref[...], preferred_element_type=F32)
        na_p = jnp.dot(na_ref[rows, :], wna_ref[...], preferred_element_type=F32)
        merged = _sigmoid(gr_ref[rows, :].astype(F32)) * ret + _sigmoid(gn_ref[rows, :].astype(F32)) * na_p
        y = alpha * x_ref[rows, :] + jnp.dot(merged.astype(BF16), wmix_ref[...], preferred_element_type=F32)
        x1 = _layer_norm_rows(y, g_ref[...], b_ref[...])
        x1e_ref[rows, :D_MODEL] = x1
        logits = jnp.dot(x1.astype(BF16), wr_ref[...], preferred_element_type=F32) + br_ref[...]
        x1e_ref[rows, D_MODEL:] = _route_record(logits)


def _post(x2, o_ret, na, rest, wret, wna, wmix, ln_g, ln_b, wr, br, alpha):
    n = x2.shape[0]
    tm = min(512, n)
    est = (2 * tm * D_MODEL * (4 + 2 + 2 + 2) + 2 * tm * MOE_ROW_W * 4 + 2 * 2 * tm * RET_V_W * 2
           + tm * RET_V_W * 2 + (RET_V_W + 2 * D_MODEL + ROUTER_LANES) * D_MODEL * 2)
    row = lambda i: (i, 0)
    whole = pl.BlockSpec(memory_space=pltpu.VMEM)
    return pl.pallas_call(
        functools.partial(_post_kernel, alpha=alpha, sub_rows=min(POST_SUB_ROWS, tm)),
        out_shape=jax.ShapeDtypeStruct((n, MOE_ROW_W), F32),
        grid=(n // tm,),
        in_specs=[
            pl.BlockSpec((tm, D_MODEL), row),
            pl.BlockSpec((tm, RET_V_W), row),
            pl.BlockSpec((tm, RET_V_W), lambda i: (i, CB_GR * COL_TILE // RET_V_W)),
            pl.BlockSpec((tm, NA_W), row),
            pl.BlockSpec((tm, D_MODEL), lambda i: (i, CB_GATE_R)),
            pl.BlockSpec((tm, D_MODEL), lambda i: (i, CB_GATE_N)),
            whole, whole, whole, whole, whole, whole, whole,
        ],
        out_specs=pl.BlockSpec((tm, MOE_ROW_W), row),
        compiler_params=pltpu.CompilerParams(
            dimension_semantics=("arbitrary",), vmem_limit_bytes=_vmem_limit(est)),
        name="post",
    )(x2, o_ret, rest, na, rest, rest, wret, wna, wmix, ln_g, ln_b, wr, br)


def _route_record(logits):
    gl = [logits[:, i:i + 1] for i in range(N_GROUPS)]
    gmax = functools.reduce(jnp.maximum, gl)
    denom = functools.reduce(lambda a, b: a + b, [jnp.exp(v - gmax) for v in gl])
    group_p = 1.0 / denom
    gidx = jnp.full(gmax.shape, N_GROUPS - 1, jnp.int32)
    for i in range(N_GROUPS - 2, -1, -1):
        gidx = jnp.where(gl[i] == gmax, i, gidx)
    el = []
    for e in range(EXPERTS_PER_GROUP):
        v = logits[:, N_GROUPS + (N_GROUPS - 1) * EXPERTS_PER_GROUP + e:N_GROUPS + (N_GROUPS - 1) * EXPERTS_PER_GROUP + e + 1]
        for gi in range(N_GROUPS - 2, -1, -1):
            c = N_GROUPS + gi * EXPERTS_PER_GROUP + e
            v = jnp.where(gidx == gi, logits[:, c:c + 1], v)
        el.append(v)
    m1 = functools.reduce(jnp.maximum, el)
    i1 = jnp.full(m1.shape, EXPERTS_PER_GROUP - 1, jnp.int32)
    for e in range(EXPERTS_PER_GROUP - 2, -1, -1):
        i1 = jnp.where(el[e] == m1, e, i1)
    neg = jnp.float32(-jnp.inf)
    rest = [jnp.where(i1 == e, neg, el[e]) for e in range(EXPERTS_PER_GROUP)]
    m2 = functools.reduce(jnp.maximum, rest)
    i2 = jnp.full(m2.shape, EXPERTS_PER_GROUP - 1, jnp.int32)
    for e in range(EXPERTS_PER_GROUP - 2, -1, -1):
        i2 = jnp.where((rest[e] == m2) & (i1 != e), e, i2)
    e2 = jnp.exp(m2 - m1)
    w1 = group_p / (1.0 + e2)
    w2 = group_p * e2 / (1.0 + e2)
    lane = lax.broadcasted_iota(jnp.int32, logits.shape, 1)
    rec = jnp.where(lane == 0, gidx.astype(F32), 0.0)
    for e in range(EXPERTS_PER_GROUP):
        gate = jnp.where(i1 == e, w1, 0.0) + jnp.where(i2 == e, w2, 0.0)
        rec = jnp.where(lane == 1 + e, gate, rec)
    return rec


def _moe_plan(gidx, tm):
    n = gidx.shape[0]
    n_tiles = n // tm + N_GROUPS
    groups = jnp.arange(N_GROUPS, dtype=jnp.int32)
    cnt = jnp.sum((gidx[:, None] == groups[None, :]).astype(jnp.int32), axis=0)
    tiles = (cnt + tm - 1) // tm
    tile_end = jnp.cumsum(tiles)
    tile_start = tile_end - tiles
    seg_start = jnp.cumsum(cnt) - cnt
    order = jnp.argsort(gidx, stable=True).astype(jnp.int32)
    t_ids = jnp.arange(n_tiles, dtype=jnp.int32)
    tgrp = jnp.minimum(jnp.sum((t_ids[:, None] >= tile_end[None, :]).astype(jnp.int32), axis=1), N_GROUPS - 1)
    first_rank = (t_ids - tile_start[tgrp]) * tm
    tcnt = jnp.clip(cnt[tgrp] - first_rank, 0, tm)
    p_local = jnp.arange(tm, dtype=jnp.int32)
    sorted_pos = seg_start[tgrp][:, None] + first_rank[:, None] + p_local[None, :]
    src = jnp.where(p_local[None, :] < tcnt[:, None], order[jnp.clip(sorted_pos, 0, n - 1)], 0)
    return src.reshape(n_tiles, 1, tm), tgrp.astype(jnp.int32), tcnt.astype(jnp.int32)


def _moe_kernel(tgrp_ref, tcnt_ref, src_ref, src_next_ref, x_hbm, w13_ref, w2_ref, g_ref, b_ref, out_hbm, xbuf,
                obuf, pad_sink, gsem, ssem, *, alpha, tm):
    del tgrp_ref
    t = pl.program_id(0)
    nt = pl.num_programs(0)
    slot = t % 2

    def start_gather(idx_ref, sl):
        def body(p, carry):
            tok = idx_ref[0, 0, p]
            pltpu.make_async_copy(x_hbm.at[pl.ds(tok, 1), :], xbuf.at[sl, pl.ds(p, 1), :], gsem.at[sl]).start()
            return carry
        for p in range(tm):
            body(p, 0)

    def wait_gather(sl):
        pltpu.make_async_copy(x_hbm.at[pl.ds(0, tm), :], xbuf.at[sl], gsem.at[sl]).wait()

    def wait_scatter(sl):
        pltpu.make_async_copy(obuf.at[sl], out_hbm.at[pl.ds(0, tm), :], ssem.at[sl]).wait()

    @pl.when(t == 0)
    def _():
        start_gather(src_ref, 0)

    wait_gather(slot)

    @pl.when(t + 1 < nt)
    def _():
        start_gather(src_next_ref, 1 - slot)

    @pl.when(t >= 2)
    def _():
        wait_scatter(slot)

    xe = xbuf[slot]
    x = xe[:, :D_MODEL]
    hh = jnp.dot(x.astype(BF16), w13_ref[0], preferred_element_type=F32)
    hs = []
    for e in range(EXPERTS_PER_GROUP):
        a = hh[:, 2 * e * D_FF_EXPERT:(2 * e + 1) * D_FF_EXPERT]
        b = hh[:, (2 * e + 1) * D_FF_EXPERT:(2 * e + 2) * D_FF_EXPERT]
        gate = xe[:, D_MODEL + 1 + e:D_MODEL + 2 + e]
        hs.append((a * _sigmoid(a) * b * gate).astype(BF16))
    y = alpha * x + jnp.dot(jnp.concatenate(hs, axis=1), w2_ref[0], preferred_element_type=F32)
    obuf[slot] = _layer_norm_rows(y, g_ref[...], b_ref[...])

    n_real = tcnt_ref[t]

    def scatter_body(p, carry):
        tok = src_ref[0, 0, p]
        pltpu.make_async_copy(obuf.at[slot, pl.ds(p, 1), :], out_hbm.at[pl.ds(tok, 1), :], ssem.at[slot]).start()
        return carry

    def sink_body(p, carry):
        pltpu.make_async_copy(obuf.at[slot, pl.ds(p, 1), :], pad_sink.at[slot, pl.ds(p, 1), :], ssem.at[slot]).start()
        return carry

    @pl.when(n_real == tm)
    def _():
        for p in range(tm):
            scatter_body(p, 0)

    @pl.when(n_real < tm)
    def _():
        lax.fori_loop(0, n_real, scatter_body, 0)
        lax.fori_loop(n_real, tm, sink_body, 0)

    @pl.when(t == nt - 1)
    def _():
        wait_scatter(slot)
        wait_scatter(1 - slot)


def _moe(x1e, w13g, w2g, ln_g, ln_b, alpha):
    n = x1e.shape[0]
    tm = min(MOE_TILE, n)
    gidx = x1e[:, D_MODEL].astype(jnp.int32)
    src, tgrp, tcnt = _moe_plan(gidx, tm)
    n_tiles = src.shape[0]
    assert n_tiles >= 2
    est = (2 * tm * MOE_ROW_W * 4 + 4 * tm * D_MODEL * 4
           + 2 * (D_MODEL * 2 * EXPERTS_PER_GROUP * D_FF_EXPERT + EXPERTS_PER_GROUP * D_FF_EXPERT * D_MODEL) * 2
           + tm * 2 * EXPERTS_PER_GROUP * D_FF_EXPERT * 4)
    idx_block = (1, 1, tm)
    grid_spec = pltpu.PrefetchScalarGridSpec(
        num_scalar_prefetch=2,
        grid=(n_tiles,),
        in_specs=[
            pl.BlockSpec(idx_block, lambda t, tg, tc: (t, 0, 0), memory_space=pltpu.SMEM),
            pl.BlockSpec(idx_block, lambda t, tg, tc: (jnp.minimum(t + 1, n_tiles - 1), 0, 0),
                         memory_space=pltpu.SMEM),
            pl.BlockSpec(memory_space=pl.ANY),
            pl.BlockSpec((1, D_MODEL, 2 * EXPERTS_PER_GROUP * D_FF_EXPERT), lambda t, tg, tc: (tg[t], 0, 0)),
            pl.BlockSpec((1, EXPERTS_PER_GROUP * D_FF_EXPERT, D_MODEL), lambda t, tg, tc: (tg[t], 0, 0)),
            pl.BlockSpec((1, D_MODEL), lambda t, tg, tc: (0, 0)),
            pl.BlockSpec((1, D_MODEL), lambda t, tg, tc: (0, 0)),
        ],
        out_specs=pl.BlockSpec(memory_space=pl.ANY),
        scratch_shapes=[
            pltpu.VMEM((2, tm, MOE_ROW_W), F32),
            pltpu.VMEM((2, tm, D_MODEL), F32),
            pltpu.VMEM((2, tm, D_MODEL), F32),
            pltpu.SemaphoreType.DMA((2,)),
            pltpu.SemaphoreType.DMA((2,)),
        ],
    )
    return pl.pallas_call(
        functools.partial(_moe_kernel, alpha=alpha, tm=tm),
        out_shape=jax.ShapeDtypeStruct((n, D_MODEL), F32),
        grid_spec=grid_spec,
        compiler_params=pltpu.CompilerParams(
            dimension_semantics=("arbitrary",), vmem_limit_bytes=_vmem_limit(est)),
        name="moe",
    )(tgrp, tcnt, src, src, x1e, w13g, w2g, ln_g, ln_b)


def _rotary_tables(seq_len):
    half = RET_DK // 2
    inv_freq = ROPE_BASE ** (-jnp.arange(half, dtype=F32) / half)
    ang = jnp.arange(seq_len, dtype=F32)[:, None] * inv_freq[None, :]
    return jnp.cos(ang), jnp.sin(ang)


def _prepare_layer(w_in, decay_fwd, decay_bwd, rpb, w_ret_out, w_na_out, w_mix_out, ln1_g, ln1_b, wg, bg, we, be, w1,
                   w3, w2, ln2_g, ln2_b):
    pad = ROUTER_LANES - N_GROUPS - N_EXPERTS
    wr = jnp.concatenate([wg, we.reshape(D_MODEL, N_EXPERTS), jnp.zeros((D_MODEL, pad), F32)], axis=1)
    br = jnp.concatenate([bg.astype(F32), be.reshape(N_EXPERTS).astype(F32), jnp.zeros((pad,), F32)])
    col_scale = jnp.ones((IN_COLS,), F32)
    col_scale = col_scale.at[RET_QK_W:2 * RET_QK_W].set(RET_DK ** -0.5)
    qn0 = QK_COLS + CB_QN * COL_TILE
    col_scale = col_scale.at[qn0:qn0 + NA_W].set(NA_HEAD_DIM ** -0.5)
    w_in = (w_in.astype(F32) * col_scale).astype(BF16)
    return dict(
        w_qk=w_in[:, :QK_COLS], w_rest=w_in[:, QK_COLS:],
        lg=jnp.stack([jax.nn.log_sigmoid(decay_fwd.astype(F32)), jax.nn.log_sigmoid(decay_bwd.astype(F32))]),
        bias_tbl=_na_bias_table(rpb),
        w_ret=w_ret_out.astype(BF16), w_na=w_na_out.astype(BF16), w_mix=w_mix_out.astype(BF16),
        ln1_g=ln1_g.astype(F32).reshape(1, D_MODEL), ln1_b=ln1_b.astype(F32).reshape(1, D_MODEL),
        wr=wr.astype(BF16), br=br.reshape(1, ROUTER_LANES),
        w13g=jnp.concatenate([w1, w3], axis=-1).astype(BF16)
        .reshape(N_GROUPS, EXPERTS_PER_GROUP, D_MODEL, 2 * D_FF_EXPERT).transpose(0, 2, 1, 3)
        .reshape(N_GROUPS, D_MODEL, 2 * EXPERTS_PER_GROUP * D_FF_EXPERT),
        w2g=w2.astype(BF16).reshape(N_GROUPS, EXPERTS_PER_GROUP * D_FF_EXPERT, D_MODEL),
        ln2_g=ln2_g.astype(F32).reshape(1, D_MODEL), ln2_b=ln2_b.astype(F32).reshape(1, D_MODEL),
    )


def _encoder_layer(x2, p, batch, seq_len, cos, sin, alpha, chunk):
    qk, rest = _in_proj(x2, p["w_qk"], p["w_rest"], cos, sin, seq_len)
    oa = _ret_fwd(qk, rest, p["lg"], batch, seq_len, chunk)
    o_ret = _ret_bwd(qk, rest, oa, p["lg"], batch, seq_len, chunk)
    na = _na(rest, p["bias_tbl"], batch, seq_len)
    x1e = _post(x2, o_ret, na, rest, p["w_ret"], p["w_na"], p["w_mix"], p["ln1_g"], p["ln1_b"], p["wr"], p["br"],
                alpha)
    return _moe(x1e, p["w13g"], p["w2g"], p["ln2_g"], p["ln2_b"], alpha)


def _trunk(x, layers, alpha, chunk=RET_CHUNK):
    batch, seq_len, _ = x.shape
    cos, sin = _rotary_tables(seq_len)
    x2 = x.reshape(batch * seq_len, D_MODEL)
    for p in layers:
        x2 = _encoder_layer(x2, p, batch, seq_len, cos, sin, alpha, chunk)
    return x2.reshape(batch, seq_len, D_MODEL)


def kernel(x_prompt, x_sample, w_in, ret_decay_fwd, ret_decay_bwd, na_rel_bias, w_ret_out, w_na_out, w_mix_out, ln1_g,
           ln1_b, router_group_w, router_group_b, router_expert_w, router_expert_b, expert_w1, expert_w3, expert_w2,
           ln2_g, ln2_b):
    depth = w_in.shape[0]
    alpha = (2 * depth) ** 0.25
    per_layer = (w_in, ret_decay_fwd, ret_decay_bwd, na_rel_bias, w_ret_out, w_na_out, w_mix_out, ln1_g, ln1_b,
                 router_group_w, router_group_b, router_expert_w, router_expert_b, expert_w1, expert_w3, expert_w2,
                 ln2_g, ln2_b)
    layers = [_prepare_layer(*[a[l] for a in per_layer]) for l in range(depth)]
    return (_trunk(x_prompt, layers, alpha), _trunk(x_sample, layers, alpha))
```

```python
import functools

import numpy as np
import jax
import jax.numpy as jnp
from jax import lax
from jax.experimental import pallas as pl
from jax.experimental.pallas import tpu as pltpu

F32 = jnp.float32
BF16 = jnp.bfloat16

D_MODEL = 1024
GRID_W = 64
RET_HEADS = 4
RET_DK = 256
RET_DV = 512
RET_QK_W = RET_HEADS * RET_DK
RET_V_W = RET_HEADS * RET_DV
ROPE_BASE = 10000.0
RET_CHUNK = 256
NA_HEADS = 16
NA_HEAD_DIM = 64
NA_W = NA_HEADS * NA_HEAD_DIM
NA_WIN_ROWS = 8
NA_WIN_COLS = 16
IN_COLS = 2 * RET_QK_W + 2 * RET_V_W + 3 * NA_W + 2 * D_MODEL
N_GROUPS = 4
EXPERTS_PER_GROUP = 4
N_EXPERTS = N_GROUPS * EXPERTS_PER_GROUP
D_FF_EXPERT = 256
LN_EPS = 1e-5
GN_EPS = 1e-5

COL_TILE = 1024
QK_COLS = 2 * RET_QK_W
REST_COLS = IN_COLS - QK_COLS
CB_QR, CB_KR = 0, 1
CB_VR, CB_GR, CB_QN, CB_KN, CB_VN, CB_GATE_R, CB_GATE_N = 0, 2, 4, 5, 6, 7, 8

ROUTER_LANES = 128
POST_SUB_ROWS = 256
MOE_ROW_W = D_MODEL + ROUTER_LANES
MOE_TILE = 512
IN_PROJ_ROWS = 2048
NA_ROWS_PER_STEP = 8
NA_HEADS_PER_GROUP = 4
NA_GROUP_W = NA_HEADS_PER_GROUP * NA_HEAD_DIM
NA_MASK_VALUE = -1e30
V7X_VMEM_BYTES = 64 * 1024 * 1024


def _vmem_limit(estimate_bytes):
    return int(min(V7X_VMEM_BYTES - 8 * 1024 * 1024, max(32 * 1024 * 1024, estimate_bytes * 3 // 2)))


def _sigmoid(x):
    return 1.0 / (1.0 + jnp.exp(-x))


def _layer_norm_rows(y, g, b):
    mu = jnp.mean(y, axis=-1, keepdims=True)
    d = y - mu
    var = jnp.mean(d * d, axis=-1, keepdims=True)
    return d * lax.rsqrt(var + LN_EPS) * g + b


def _in_proj_rotary_kernel(x_ref, w_ref, cos_ref, sin_ref, o_ref, xb_ref):
    @pl.when(pl.program_id(1) == 0)
    def _():
        xb_ref[...] = x_ref[...].astype(BF16)

    acc = jnp.dot(xb_ref[...], w_ref[...], preferred_element_type=F32)
    cos = cos_ref[...]
    sin = sin_ref[...]
    half = RET_DK // 2
    for h in range(RET_HEADS):
        x1 = acc[:, h * RET_DK:h * RET_DK + half]
        x2 = acc[:, h * RET_DK + half:(h + 1) * RET_DK]
        o_ref[:, h * RET_DK:h * RET_DK + half] = (x1 * cos - x2 * sin).astype(BF16)
        o_ref[:, h * RET_DK + half:(h + 1) * RET_DK] = (x1 * sin + x2 * cos).astype(BF16)


def _in_proj_plain_kernel(x_ref, w_ref, o_ref, xb_ref):
    @pl.when(pl.program_id(1) == 0)
    def _():
        xb_ref[...] = x_ref[...].astype(BF16)

    o_ref[...] = jnp.dot(xb_ref[...], w_ref[...], preferred_element_type=F32).astype(BF16)


def _in_proj(x2, w_in, cos, sin, seq_len):
    n = x2.shape[0]
    tm = min(IN_PROJ_ROWS, seq_len)
    tiles_per_seq = seq_len // tm
    qk_blocks = QK_COLS // COL_TILE
    est = (2 * tm * D_MODEL * 4 + tm * D_MODEL * 2 + 2 * D_MODEL * COL_TILE * 2 + 2 * tm * COL_TILE * 2
           + tm * COL_TILE * 4)
    params = pltpu.CompilerParams(dimension_semantics=("arbitrary", "arbitrary"), vmem_limit_bytes=_vmem_limit(est))
    x_spec = pl.BlockSpec((tm, D_MODEL), lambda i, j: (i, 0))
    w_spec = pl.BlockSpec((D_MODEL, COL_TILE), lambda i, j: (0, j))
    o_spec = pl.BlockSpec((tm, COL_TILE), lambda i, j: (i, j))
    rot_spec = pl.BlockSpec((tm, RET_DK // 2), lambda i, j: (i % tiles_per_seq, 0))
    qk = pl.pallas_call(
        _in_proj_rotary_kernel,
        out_shape=jax.ShapeDtypeStruct((n, QK_COLS), BF16),
        grid=(n // tm, QK_COLS // COL_TILE),
        in_specs=[x_spec, w_spec, rot_spec, rot_spec],
        out_specs=o_spec,
        scratch_shapes=[pltpu.VMEM((tm, D_MODEL), BF16)],
        compiler_params=params,
        name="in_proj_qk",
    )(x2, w_in, cos, sin)
    rest = pl.pallas_call(
        _in_proj_plain_kernel,
        out_shape=jax.ShapeDtypeStruct((n, REST_COLS), BF16),
        grid=(n // tm, REST_COLS // COL_TILE),
        in_specs=[x_spec, pl.BlockSpec((D_MODEL, COL_TILE), lambda i, j: (0, j + qk_blocks))],
        out_specs=o_spec,
        scratch_shapes=[pltpu.VMEM((tm, D_MODEL), BF16)],
        compiler_params=params,
        name="in_proj_rest",
    )(x2, w_in)
    return qk, rest


def _ret_tables(lg_ref, chunk, dmat_ref, qd_ref, kd_ref, cd_ref, direction):
    ii = lax.broadcasted_iota(jnp.int32, (chunk, chunk), 0)
    jj = lax.broadcasted_iota(jnp.int32, (chunk, chunk), 1)
    diff = (ii - jj).astype(F32)
    pos_q = lax.broadcasted_iota(jnp.int32, (chunk, RET_DV), 0).astype(F32)
    pos_k = lax.broadcasted_iota(jnp.int32, (chunk, RET_DK), 0).astype(F32)
    for h in range(RET_HEADS):
        lf = lg_ref[0, h]
        lb = lg_ref[1, h]
        if dmat_ref is not None:
            dmat_ref[h] = jnp.where(diff >= 0, jnp.exp(jnp.maximum(diff, 0.0) * lf),
                                    jnp.exp(jnp.maximum(-diff, 0.0) * lb))
        if direction == 0:
            qd_ref[h] = jnp.exp((pos_q + 1.0) * lf)
            kd_ref[h] = jnp.exp((chunk - 1.0 - pos_k) * lf)
            cd_ref[h] = jnp.exp(jnp.zeros((8, RET_DV), F32) + chunk * lf)
        else:
            qd_ref[h] = jnp.exp((chunk - pos_q) * lb)
            kd_ref[h] = jnp.exp(pos_k * lb)
            cd_ref[h] = jnp.exp(jnp.zeros((8, RET_DV), F32) + chunk * lb)


def _ret_cross_and_update(q, k, v, h, state_ref, qd_ref, kd_ref, cd_ref):
    st = state_ref[h]
    cross = jnp.dot(q, st.astype(BF16), preferred_element_type=F32) * qd_ref[h]
    kd = (k.astype(F32) * kd_ref[h]).astype(BF16)
    upd = lax.dot_general(kd, v, (((0,), (0,)), ((), ())), preferred_element_type=F32)
    state_ref[h] = st * cd_ref[h, 0:1, :] + upd
    return cross


def _ret_fwd_kernel(lg_ref, q_ref, k_ref, v_ref, o_ref, state_ref, dmat_ref, qd_ref, kd_ref, cd_ref, *, chunk,
                    n_chunks):
    first = (pl.program_id(0) == 0) & (pl.program_id(1) == 0)

    @pl.when(first)
    def _():
        _ret_tables(lg_ref, chunk, dmat_ref, qd_ref, kd_ref, cd_ref, 0)

    @pl.when(pl.program_id(1) == 0)
    def _():
        state_ref[...] = jnp.zeros(state_ref.shape, F32)

    def body(c, carry):
        rows = pl.ds(pl.multiple_of(c * chunk, chunk), chunk)
        for h in range(RET_HEADS):
            q = q_ref[rows, h * RET_DK:(h + 1) * RET_DK]
            k = k_ref[rows, h * RET_DK:(h + 1) * RET_DK]
            v = v_ref[rows, h * RET_DV:(h + 1) * RET_DV]
            s = lax.dot_general(q, k, (((1,), (1,)), ((), ())), preferred_element_type=F32)
            p = (s * dmat_ref[h]).astype(BF16)
            o = jnp.dot(p, v, preferred_element_type=F32)
            o = o + _ret_cross_and_update(q, k, v, h, state_ref, qd_ref, kd_ref, cd_ref)
            o_ref[rows, h * RET_DV:(h + 1) * RET_DV] = o.astype(o_ref.dtype)
        return carry

    lax.fori_loop(0, n_chunks, body, 0)


def _ret_bwd_kernel(lg_ref, q_ref, k_ref, v_ref, oa_ref, o_ref, state_ref, qd_ref, kd_ref, cd_ref, *, chunk, n_chunks):
    first = (pl.program_id(0) == 0) & (pl.program_id(1) == 0)

    @pl.when(first)
    def _():
        _ret_tables(lg_ref, chunk, None, qd_ref, kd_ref, cd_ref, 1)

    @pl.when(pl.program_id(1) == 0)
    def _():
        state_ref[...] = jnp.zeros(state_ref.shape, F32)

    def body(cc, carry):
        c = n_chunks - 1 - cc
        rows = pl.ds(pl.multiple_of(c * chunk, chunk), chunk)
        for h in range(RET_HEADS):
            q = q_ref[rows, h * RET_DK:(h + 1) * RET_DK]
            k = k_ref[rows, h * RET_DK:(h + 1) * RET_DK]
            v = v_ref[rows, h * RET_DV:(h + 1) * RET_DV]
            o = oa_ref[rows, h * RET_DV:(h + 1) * RET_DV].astype(F32)
            o = o + _ret_cross_and_update(q, k, v, h, state_ref, qd_ref, kd_ref, cd_ref)
            o_ref[rows, h * RET_DV:(h + 1) * RET_DV] = o.astype(o_ref.dtype)
        return carry

    lax.fori_loop(0, n_chunks, body, 0)


def _ret_step_tokens(seq_len, chunk):
    return min(512, seq_len) // chunk * chunk


def _ret_fwd(qk, rest, lg, batch, seq_len, chunk):
    ts = _ret_step_tokens(seq_len, chunk)
    nt = seq_len // ts
    n = batch * seq_len
    est = (2 * (2 * ts * RET_QK_W * 2 + ts * RET_V_W * 2 + ts * RET_V_W * 2)
           + RET_HEADS * (RET_DK * RET_DV * 4 + chunk * chunk * 4 + chunk * RET_DV * 4 + chunk * RET_DK * 4))
    kern = functools.partial(_ret_fwd_kernel, chunk=chunk, n_chunks=ts // chunk)
    return pl.pallas_call(
        kern,
        out_shape=jax.ShapeDtypeStruct((n, RET_V_W), BF16),
        grid=(batch, nt),
        in_specs=[
            pl.BlockSpec(memory_space=pltpu.SMEM),
            pl.BlockSpec((ts, RET_QK_W), lambda b, t: (b * nt + t, CB_QR)),
            pl.BlockSpec((ts, RET_QK_W), lambda b, t: (b * nt + t, CB_KR)),
            pl.BlockSpec((ts, RET_V_W), lambda b, t: (b * nt + t, CB_VR * COL_TILE // RET_V_W)),
        ],
        out_specs=pl.BlockSpec((ts, RET_V_W), lambda b, t: (b * nt + t, 0)),
        scratch_shapes=[
            pltpu.VMEM((RET_HEADS, RET_DK, RET_DV), F32),
            pltpu.VMEM((RET_HEADS, chunk, chunk), F32),
            pltpu.VMEM((RET_HEADS, chunk, RET_DV), F32),
            pltpu.VMEM((RET_HEADS, chunk, RET_DK), F32),
            pltpu.VMEM((RET_HEADS, 8, RET_DV), F32),
        ],
        compiler_params=pltpu.CompilerParams(
            dimension_semantics=("arbitrary", "arbitrary"), vmem_limit_bytes=_vmem_limit(est)),
        name="ret_fwd",
    )(lg, qk, qk, rest)


def _ret_bwd(qk, rest, oa, lg, batch, seq_len, chunk):
    ts = _ret_step_tokens(seq_len, chunk)
    nt = seq_len // ts
    n = batch * seq_len
    est = (2 * (2 * ts * RET_QK_W * 2 + 3 * ts * RET_V_W * 2)
           + RET_HEADS * (RET_DK * RET_DV * 4 + chunk * RET_DV * 4 + chunk * RET_DK * 4))
    kern = functools.partial(_ret_bwd_kernel, chunk=chunk, n_chunks=ts // chunk)

    def rev(b, t):
        return b * nt + (nt - 1 - t)

    return pl.pallas_call(
        kern,
        out_shape=jax.ShapeDtypeStruct((n, RET_V_W), BF16),
        grid=(batch, nt),
        in_specs=[
            pl.BlockSpec(memory_space=pltpu.SMEM),
            pl.BlockSpec((ts, RET_QK_W), lambda b, t: (rev(b, t), CB_QR)),
            pl.BlockSpec((ts, RET_QK_W), lambda b, t: (rev(b, t), CB_KR)),
            pl.BlockSpec((ts, RET_V_W), lambda b, t: (rev(b, t), CB_VR * COL_TILE // RET_V_W)),
            pl.BlockSpec((ts, RET_V_W), lambda b, t: (rev(b, t), 0)),
        ],
        out_specs=pl.BlockSpec((ts, RET_V_W), lambda b, t: (rev(b, t), 0)),
        scratch_shapes=[
            pltpu.VMEM((RET_HEADS, RET_DK, RET_DV), F32),
            pltpu.VMEM((RET_HEADS, chunk, RET_DV), F32),
            pltpu.VMEM((RET_HEADS, chunk, RET_DK), F32),
            pltpu.VMEM((RET_HEADS, 8, RET_DV), F32),
        ],
        compiler_params=pltpu.CompilerParams(
            dimension_semantics=("arbitrary", "arbitrary"), vmem_limit_bytes=_vmem_limit(est)),
        name="ret_bwd",
    )(lg, qk, qk, rest, oa)


def _na_bias_table(rpb):
    cols = np.arange(GRID_W)
    col_start = np.clip(cols - NA_WIN_COLS // 2, 0, GRID_W - NA_WIN_COLS)
    valid = (cols[None, :] >= col_start[:, None]) & (cols[None, :] < col_start[:, None] + NA_WIN_COLS)
    pad = GRID_W
    padded = jnp.pad(rpb.astype(F32), ((0, 0), (0, 0), (pad, pad)))
    tb = jnp.stack([padded[:, :, pad + NA_WIN_COLS - 1 - c:pad + NA_WIN_COLS - 1 - c + GRID_W] for c in range(GRID_W)],
                   axis=2)
    tb = jnp.where(valid[None, None], tb, NA_MASK_VALUE)
    var = jnp.stack([tb[:, s:s + NA_WIN_ROWS] for s in range(NA_WIN_ROWS)], 0)
    return var.transpose(0, 1, 3, 2, 4).reshape(NA_WIN_ROWS, NA_HEADS * GRID_W, NA_WIN_ROWS * GRID_W)


def _na_kernel(q_ref, kp_ref, kc_ref, kn_ref, vp_ref, vc_ref, vn_ref, bias_ref, o_ref, kbuf, vbuf, *, grid_rows):
    t = pl.program_id(1)
    blk = NA_ROWS_PER_STEP * GRID_W
    win = NA_WIN_ROWS * GRID_W
    kbuf[0:blk] = kp_ref[...]
    kbuf[blk:2 * blk] = kc_ref[...]
    kbuf[2 * blk:3 * blk] = kn_ref[...]
    vbuf[0:blk] = vp_ref[...]
    vbuf[blk:2 * blk] = vc_ref[...]
    vbuf[2 * blk:3 * blk] = vn_ref[...]

    rr = lax.broadcasted_iota(jnp.int32, (NA_GROUP_W, NA_GROUP_W), 0) // GRID_W
    ll = lax.broadcasted_iota(jnp.int32, (NA_GROUP_W, NA_GROUP_W), 1) // NA_HEAD_DIM
    head_mask = rr == ll

    def body(i, carry):
        r = t * NA_ROWS_PER_STEP + i
        r0 = jnp.clip(r - NA_WIN_ROWS // 2, 0, grid_rows - NA_WIN_ROWS)
        shift = r0 - r + NA_WIN_ROWS - 1
        off = pl.multiple_of((r0 - (t - 1) * NA_ROWS_PER_STEP) * GRID_W, GRID_W)
        qrows = pl.ds(pl.multiple_of(i * GRID_W, GRID_W), GRID_W)
        groups = [slice(g * NA_GROUP_W, (g + 1) * NA_GROUP_W) for g in range(NA_HEADS // NA_HEADS_PER_GROUP)]
        scores = []
        for lanes in groups:
            q4 = q_ref[qrows, lanes]
            qm = jnp.where(head_mask, jnp.concatenate([q4] * NA_HEADS_PER_GROUP, axis=0), jnp.zeros((), BF16))
            k4 = kbuf[pl.ds(off, win), lanes]
            sc = lax.dot_general(qm, k4, (((1,), (1,)), ((), ())), preferred_element_type=F32)
            scores.append(sc + bias_ref[shift, lanes, :])
        probs = []
        for sc in scores:
            p = jnp.exp(sc - jnp.max(sc, axis=-1, keepdims=True))
            probs.append((p.astype(BF16), jnp.sum(p, axis=-1, keepdims=True)))
        for lanes, (pb, l) in zip(groups, probs):
            v4 = vbuf[pl.ds(off, win), lanes]
            oall = jnp.dot(pb, v4, preferred_element_type=F32) * (1.0 / l)
            oall = jnp.where(head_mask, oall, 0.0)
            o4 = oall[0:GRID_W]
            for hh in range(1, NA_HEADS_PER_GROUP):
                o4 = o4 + oall[hh * GRID_W:(hh + 1) * GRID_W]
            o_ref[qrows, lanes] = o4.astype(o_ref.dtype)
        return carry

    lax.fori_loop(0, NA_ROWS_PER_STEP, body, 0, unroll=2)


def _na(rest, bias_tbl, batch, seq_len):
    grid_rows = seq_len // GRID_W
    assert grid_rows % NA_ROWS_PER_STEP == 0 and grid_rows >= NA_WIN_ROWS
    nb = grid_rows // NA_ROWS_PER_STEP
    blk = NA_ROWS_PER_STEP * GRID_W
    n = batch * seq_len
    est = (2 * 7 * blk * NA_W * 2 + 2 * blk * NA_W * 2 + 2 * 3 * blk * NA_W * 2
           + NA_WIN_ROWS * NA_W * NA_WIN_ROWS * GRID_W * 4)

    def cur(b, t):
        return b * nb + t

    def prev(b, t):
        return b * nb + jnp.maximum(t - 1, 0)

    def nxt(b, t):
        return b * nb + jnp.minimum(t + 1, nb - 1)

    kern = functools.partial(_na_kernel, grid_rows=grid_rows)
    return pl.pallas_call(
        kern,
        out_shape=jax.ShapeDtypeStruct((n, NA_W), BF16),
        grid=(batch, nb),
        in_specs=[
            pl.BlockSpec((blk, NA_W), lambda b, t: (cur(b, t), CB_QN)),
            pl.BlockSpec((blk, NA_W), lambda b, t: (prev(b, t), CB_KN)),
            pl.BlockSpec((blk, NA_W), lambda b, t: (cur(b, t), CB_KN)),
            pl.BlockSpec((blk, NA_W), lambda b, t: (nxt(b, t), CB_KN)),
            pl.BlockSpec((blk, NA_W), lambda b, t: (prev(b, t), CB_VN)),
            pl.BlockSpec((blk, NA_W), lambda b, t: (cur(b, t), CB_VN)),
            pl.BlockSpec((blk, NA_W), lambda b, t: (nxt(b, t), CB_VN)),
            pl.BlockSpec(memory_space=pltpu.VMEM),
        ],
        out_specs=pl.BlockSpec((blk, NA_W), lambda b, t: (cur(b, t), 0)),
        scratch_shapes=[pltpu.VMEM((3 * blk, NA_W), BF16), pltpu.VMEM((3 * blk, NA_W), BF16)],
        compiler_params=pltpu.CompilerParams(
            dimension_semantics=("arbitrary", "arbitrary"), vmem_limit_bytes=_vmem_limit(est)),
        name="na",
    )(rest, rest, rest, rest, rest, rest, rest, bias_tbl)


def _post_kernel(x_ref, o_ref, og_ref, na_ref, gr_ref, gn_ref, wret_ref, wna_ref, wmix_ref, g_ref, b_ref, wr_ref,
                 br_ref, x1e_ref, *, alpha, sub_rows):
    for r0 in range(0, x_ref.shape[0], sub_rows):
        rows = slice(r0, r0 + sub_rows)
        gated = []
        for h in range(RET_HEADS):
            lanes = slice(h * RET_DV, (h + 1) * RET_DV)
            o = o_ref[rows, lanes].astype(F32)
            mu = jnp.mean(o, axis=-1, keepdims=True)
            d = o - mu
            var = jnp.mean(d * d, axis=-1, keepdims=True)
            og = og_ref[rows, lanes].astype(F32)
            gated.append((og * _sigmoid(og) * (d * lax.rsqrt(var + GN_EPS))).astype(BF16))
        ret = jnp.dot(jnp.concatenate(gated, axis=1), wret_ref[...], preferred_element_type=F32)
        na_p = jnp.dot(na_ref[rows, :], wna_ref[...], preferred_element_type=F32)
        merged = _sigmoid(gr_ref[rows, :].astype(F32)) * ret + _sigmoid(gn_ref[rows, :].astype(F32)) * na_p
        y = alpha * x_ref[rows, :] + jnp.dot(merged.astype(BF16), wmix_ref[...], preferred_element_type=F32)
        x1 = _layer_norm_rows(y, g_ref[...], b_ref[...])
        x1e_ref[rows, :D_MODEL] = x1
        logits = jnp.dot(x1.astype(BF16), wr_ref[...], preferred_element_type=F32) + br_ref[...]
        x1e_ref[rows, D_MODEL:] = _route_record(logits)


def _post(x2, o_ret, na, rest, wret, wna, wmix, ln_g, ln_b, wr, br, alpha):
    n = x2.shape[0]
    tm = min(512, n)
    est = (2 * tm * D_MODEL * (4 + 2 + 2 + 2) + 2 * tm * MOE_ROW_W * 4 + 2 * 2 * tm * RET_V_W * 2
           + tm * RET_V_W * 2 + (RET_V_W + 2 * D_MODEL + ROUTER_LANES) * D_MODEL * 2)
    row = lambda i: (i, 0)
    whole = pl.BlockSpec(memory_space=pltpu.VMEM)
    return pl.pallas_call(
        functools.partial(_post_kernel, alpha=alpha, sub_rows=min(POST_SUB_ROWS, tm)),
        out_shape=jax.ShapeDtypeStruct((n, MOE_ROW_W), F32),
        grid=(n // tm,),
        in_specs=[
            pl.BlockSpec((tm, D_MODEL), row),
            pl.BlockSpec((tm, RET_V_W), row),
            pl.BlockSpec((tm, RET_V_W), lambda i: (i, CB_GR * COL_TILE // RET_V_W)),
            pl.BlockSpec((tm, NA_W), row),
            pl.BlockSpec((tm, D_MODEL), lambda i: (i, CB_GATE_R)),
            pl.BlockSpec((tm, D_MODEL), lambda i: (i, CB_GATE_N)),
            whole, whole, whole, whole, whole, whole, whole,
        ],
        out_specs=pl.BlockSpec((tm, MOE_ROW_W), row),
        compiler_params=pltpu.CompilerParams(
            dimension_semantics=("arbitrary",), vmem_limit_bytes=_vmem_limit(est)),
        name="post",
    )(x2, o_ret, rest, na, rest, rest, wret, wna, wmix, ln_g, ln_b, wr, br)


def _route_record(logits):
    lt = logits.T
    gl = [lt[i:i + 1, :] for i in range(N_GROUPS)]
    gmax = functools.reduce(jnp.maximum, gl)
    denom = functools.reduce(lambda a, b: a + b, [jnp.exp(v - gmax) for v in gl])
    group_p = 1.0 / denom
    gidx = jnp.full(gmax.shape, N_GROUPS - 1, jnp.int32)
    for i in range(N_GROUPS - 2, -1, -1):
        gidx = jnp.where(gl[i] == gmax, i, gidx)
    el = []
    for e in range(EXPERTS_PER_GROUP):
        c = N_GROUPS + (N_GROUPS - 1) * EXPERTS_PER_GROUP + e
        v = lt[c:c + 1, :]
        for gi in range(N_GROUPS - 2, -1, -1):
            c = N_GROUPS + gi * EXPERTS_PER_GROUP + e
            v = jnp.where(gidx == gi, lt[c:c + 1, :], v)
        el.append(v)
    m1 = functools.reduce(jnp.maximum, el)
    i1 = jnp.full(m1.shape, EXPERTS_PER_GROUP - 1, jnp.int32)
    for e in range(EXPERTS_PER_GROUP - 2, -1, -1):
        i1 = jnp.where(el[e] == m1, e, i1)
    neg = jnp.float32(-jnp.inf)
    rest = [jnp.where(i1 == e, neg, el[e]) for e in range(EXPERTS_PER_GROUP)]
    m2 = functools.reduce(jnp.maximum, rest)
    i2 = jnp.full(m2.shape, EXPERTS_PER_GROUP - 1, jnp.int32)
    for e in range(EXPERTS_PER_GROUP - 2, -1, -1):
        i2 = jnp.where((rest[e] == m2) & (i1 != e), e, i2)
    e2 = jnp.exp(m2 - m1)
    w1 = group_p / (1.0 + e2)
    w2 = group_p * e2 / (1.0 + e2)
    field = lax.broadcasted_iota(jnp.int32, lt.shape, 0)
    rec = jnp.where(field == 0, gidx.astype(F32), 0.0)
    for e in range(EXPERTS_PER_GROUP):
        gate = jnp.where(i1 == e, w1, 0.0) + jnp.where(i2 == e, w2, 0.0)
        rec = jnp.where(field == 1 + e, gate, rec)
    return rec.T


def _moe_plan(gidx, tm):
    n = gidx.shape[0]
    n_tiles = (n // tm + N_GROUPS + 1) // 2 * 2
    groups = jnp.arange(N_GROUPS, dtype=jnp.int32)
    cnt = jnp.sum((gidx[:, None] == groups[None, :]).astype(jnp.int32), axis=0)
    tiles = (cnt + tm - 1) // tm
    tile_end = jnp.cumsum(tiles)
    tile_start = tile_end - tiles
    seg_start = jnp.cumsum(cnt) - cnt
    order = jnp.argsort(gidx, stable=True).astype(jnp.int32)
    t_ids = jnp.arange(n_tiles, dtype=jnp.int32)
    tgrp = jnp.minimum(jnp.sum((t_ids[:, None] >= tile_end[None, :]).astype(jnp.int32), axis=1), N_GROUPS - 1)
    first_rank = (t_ids - tile_start[tgrp]) * tm
    tcnt = jnp.clip(cnt[tgrp] - first_rank, 0, tm)
    p_local = jnp.arange(tm, dtype=jnp.int32)
    sorted_pos = seg_start[tgrp][:, None] + first_rank[:, None] + p_local[None, :]
    src = jnp.where(p_local[None, :] < tcnt[:, None], order[jnp.clip(sorted_pos, 0, n - 1)], 0)
    return src.reshape(n_tiles, 1, tm), tgrp.astype(jnp.int32), tcnt.astype(jnp.int32)


def _moe_kernel(tgrp_ref, tcnt_ref, src_ref, src_next_ref, x_hbm, w1a_ref, w3a_ref, w2a_ref, w1b_ref, w3b_ref, w2b_ref,
                g_ref, b_ref, out_hbm, xbuf, obuf, pad_sink, gsem, ssem, *, alpha, tm):
    del tgrp_ref
    s = pl.program_id(0)
    ns = pl.num_programs(0)

    def start_gather(idx_ref, which, sl):
        for p in range(tm):
            tok = idx_ref[which, 0, p]
            pltpu.make_async_copy(x_hbm.at[pl.ds(tok, 1), :], xbuf.at[sl, pl.ds(p, 1), :], gsem.at[sl]).start()

    def wait_gather(sl):
        pltpu.make_async_copy(x_hbm.at[pl.ds(0, tm), :], xbuf.at[sl], gsem.at[sl]).wait()

    def wait_scatter(sl):
        pltpu.make_async_copy(obuf.at[sl], out_hbm.at[pl.ds(0, tm), :], ssem.at[sl]).wait()

    def compute(sl, w1_ref, w3_ref, w2_ref):
        xe = xbuf[sl]
        x = xe[:, :D_MODEL]
        xb = x.astype(BF16)
        hs = []
        for e in range(EXPERTS_PER_GROUP):
            a = jnp.dot(xb, w1_ref[e], preferred_element_type=F32)
            b = jnp.dot(xb, w3_ref[e], preferred_element_type=F32)
            gate = xe[:, D_MODEL + 1 + e:D_MODEL + 2 + e]
            hs.append((a * _sigmoid(a) * b * gate).astype(BF16))
        w2 = w2_ref[...].reshape(EXPERTS_PER_GROUP * D_FF_EXPERT, D_MODEL)
        y = alpha * x + jnp.dot(jnp.concatenate(hs, axis=1), w2, preferred_element_type=F32)
        obuf[sl] = _layer_norm_rows(y, g_ref[...], b_ref[...])

    def start_scatter(which, sl):
        n_real = tcnt_ref[2 * s + which]

        def row_copy(p):
            tok = src_ref[which, 0, p]
            return pltpu.make_async_copy(obuf.at[sl, pl.ds(p, 1), :], out_hbm.at[pl.ds(tok, 1), :], ssem.at[sl])

        def sink_copy(p):
            return pltpu.make_async_copy(obuf.at[sl, pl.ds(p, 1), :], pad_sink.at[sl, pl.ds(p, 1), :], ssem.at[sl])

        @pl.when(n_real == tm)
        def _():
            for p in range(tm):
                row_copy(p).start()

        @pl.when(n_real < tm)
        def _():
            def real_body(p, carry):
                row_copy(p).start()
                return carry

            def sink_body(p, carry):
                sink_copy(p).start()
                return carry

            lax.fori_loop(0, n_real, real_body, 0)
            lax.fori_loop(n_real, tm, sink_body, 0)

    @pl.when(s == 0)
    def _():
        start_gather(src_ref, 0, 0)

    wait_gather(0)
    start_gather(src_ref, 1, 1)

    @pl.when(s >= 1)
    def _():
        wait_scatter(0)

    compute(0, w1a_ref, w3a_ref, w2a_ref)
    start_scatter(0, 0)
    wait_gather(1)

    @pl.when(s + 1 < ns)
    def _():
        start_gather(src_next_ref, 0, 0)

    @pl.when(s >= 1)
    def _():
        wait_scatter(1)

    compute(1, w1b_ref, w3b_ref, w2b_ref)
    start_scatter(1, 1)

    @pl.when(s == ns - 1)
    def _():
        wait_scatter(0)
        wait_scatter(1)


def _moe(x1e, w1, w3, w2, ln_g, ln_b, alpha):
    n = x1e.shape[0]
    tm = min(MOE_TILE, n)
    gidx = x1e[:, D_MODEL].astype(jnp.int32)
    src, tgrp, tcnt = _moe_plan(gidx, tm)
    n_tiles = src.shape[0]
    assert n_tiles % 2 == 0
    est = (2 * tm * MOE_ROW_W * 4 + 4 * tm * D_MODEL * 4 + 2 * 2 * 3 * EXPERTS_PER_GROUP * D_MODEL * D_FF_EXPERT * 2
           + tm * 2 * EXPERTS_PER_GROUP * D_FF_EXPERT * 4)
    w_up = (EXPERTS_PER_GROUP, D_MODEL, D_FF_EXPERT)
    w_down = (EXPERTS_PER_GROUP, D_FF_EXPERT, D_MODEL)
    grid_spec = pltpu.PrefetchScalarGridSpec(
        num_scalar_prefetch=2,
        grid=(n_tiles // 2,),
        in_specs=[
            pl.BlockSpec((2, 1, tm), lambda s, tg, tc: (s, 0, 0), memory_space=pltpu.SMEM),
            pl.BlockSpec((1, 1, tm), lambda s, tg, tc: (jnp.minimum(2 * s + 2, n_tiles - 1), 0, 0),
                         memory_space=pltpu.SMEM),
            pl.BlockSpec(memory_space=pl.ANY),
            pl.BlockSpec(w_up, lambda s, tg, tc: (tg[2 * s], 0, 0)),
            pl.BlockSpec(w_up, lambda s, tg, tc: (tg[2 * s], 0, 0)),
            pl.BlockSpec(w_down, lambda s, tg, tc: (tg[2 * s], 0, 0)),
            pl.BlockSpec(w_up, lambda s, tg, tc: (tg[2 * s + 1], 0, 0)),
            pl.BlockSpec(w_up, lambda s, tg, tc: (tg[2 * s + 1], 0, 0)),
            pl.BlockSpec(w_down, lambda s, tg, tc: (tg[2 * s + 1], 0, 0)),
            pl.BlockSpec((1, D_MODEL), lambda s, tg, tc: (0, 0)),
            pl.BlockSpec((1, D_MODEL), lambda s, tg, tc: (0, 0)),
        ],
        out_specs=pl.BlockSpec(memory_space=pl.ANY),
        scratch_shapes=[
            pltpu.VMEM((2, tm, MOE_ROW_W), F32),
            pltpu.VMEM((2, tm, D_MODEL), F32),
            pltpu.VMEM((2, tm, D_MODEL), F32),
            pltpu.SemaphoreType.DMA((2,)),
            pltpu.SemaphoreType.DMA((2,)),
        ],
    )
    return pl.pallas_call(
        functools.partial(_moe_kernel, alpha=alpha, tm=tm),
        out_shape=jax.ShapeDtypeStruct((n, D_MODEL), F32),
        grid_spec=grid_spec,
        compiler_params=pltpu.CompilerParams(
            dimension_semantics=("arbitrary",), vmem_limit_bytes=_vmem_limit(est)),
        name="moe",
    )(tgrp, tcnt, src, src, x1e, w1, w3, w2, w1, w3, w2, ln_g, ln_b)


def _rotary_tables(seq_len):
    half = RET_DK // 2
    inv_freq = ROPE_BASE ** (-jnp.arange(half, dtype=F32) / half)
    ang = jnp.arange(seq_len, dtype=F32)[:, None] * inv_freq[None, :]
    return jnp.cos(ang), jnp.sin(ang)


def _prepare_layer(w_in, decay_fwd, decay_bwd, rpb, w_ret_out, w_na_out, w_mix_out, ln1_g, ln1_b, wg, bg, we, be, w1,
                   w3, w2, ln2_g, ln2_b):
    pad = ROUTER_LANES - N_GROUPS - N_EXPERTS
    wr = jnp.concatenate([wg, we.reshape(D_MODEL, N_EXPERTS), jnp.zeros((D_MODEL, pad), F32)], axis=1)
    br = jnp.concatenate([bg.astype(F32), be.reshape(N_EXPERTS).astype(F32), jnp.zeros((pad,), F32)])
    col_scale = jnp.ones((IN_COLS,), F32)
    col_scale = col_scale.at[RET_QK_W:2 * RET_QK_W].set(RET_DK ** -0.5)
    qn0 = QK_COLS + CB_QN * COL_TILE
    col_scale = col_scale.at[qn0:qn0 + NA_W].set(NA_HEAD_DIM ** -0.5)
    w_in = (w_in.astype(F32) * col_scale).astype(BF16)
    return dict(
        w_in=w_in,
        lg=jnp.stack([jax.nn.log_sigmoid(decay_fwd.astype(F32)), jax.nn.log_sigmoid(decay_bwd.astype(F32))]),
        bias_tbl=_na_bias_table(rpb),
        w_ret=w_ret_out.astype(BF16), w_na=w_na_out.astype(BF16), w_mix=w_mix_out.astype(BF16),
        ln1_g=ln1_g.astype(F32).reshape(1, D_MODEL), ln1_b=ln1_b.astype(F32).reshape(1, D_MODEL),
        wr=wr.astype(BF16), br=br.reshape(1, ROUTER_LANES),
        w1=w1.astype(BF16), w3=w3.astype(BF16), w2=w2.astype(BF16),
        ln2_g=ln2_g.astype(F32).reshape(1, D_MODEL), ln2_b=ln2_b.astype(F32).reshape(1, D_MODEL),
    )


def _encoder_layer(x2, p, batch, seq_len, cos, sin, alpha, chunk):
    qk, rest = _in_proj(x2, p["w_in"], cos, sin, seq_len)
    oa = _ret_fwd(qk, rest, p["lg"], batch, seq_len, chunk)
    o_ret = _ret_bwd(qk, rest, oa, p["lg"], batch, seq_len, chunk)
    na = _na(rest, p["bias_tbl"], batch, seq_len)
    x1e = _post(x2, o_ret, na, rest, p["w_ret"], p["w_na"], p["w_mix"], p["ln1_g"], p["ln1_b"], p["wr"], p["br"],
                alpha)
    return _moe(x1e, p["w1"], p["w3"], p["w2"], p["ln2_g"], p["ln2_b"], alpha)


def _trunk(x, layers, alpha, chunk=RET_CHUNK):
    batch, seq_len, _ = x.shape
    cos, sin = _rotary_tables(seq_len)
    x2 = x.reshape(batch * seq_len, D_MODEL)
    for p in layers:
        x2 = _encoder_layer(x2, p, batch, seq_len, cos, sin, alpha, chunk)
    return x2.reshape(batch, seq_len, D_MODEL)


def kernel(x_prompt, x_sample, w_in, ret_decay_fwd, ret_decay_bwd, na_rel_bias, w_ret_out, w_na_out, w_mix_out, ln1_g,
           ln1_b, router_group_w, router_group_b, router_expert_w, router_expert_b, expert_w1, expert_w3, expert_w2,
           ln2_g, ln2_b):
    depth = w_in.shape[0]
    alpha = (2 * depth) ** 0.25
    per_layer = (w_in, ret_decay_fwd, ret_decay_bwd, na_rel_bias, w_ret_out, w_na_out, w_mix_out, ln1_g, ln1_b,
                 router_group_w, router_group_b, router_expert_w, router_expert_b, expert_w1, expert_w3, expert_w2,
                 ln2_g, ln2_b)
    layers = [_prepare_layer(*[a[l] for a in per_layer]) for l in range(depth)]
    return (_trunk(x_prompt, layers, alpha), _trunk(x_sample, layers, alpha))
```

```python
import functools

import numpy as np
import jax
import jax.numpy as jnp
from jax import lax
from jax.experimental import pallas as pl
from jax.experimental.pallas import tpu as pltpu

F32 = jnp.float32
BF16 = jnp.bfloat16

D_MODEL = 1024
GRID_W = 64
RET_HEADS = 4
RET_DK = 256
RET_DV = 512
RET_QK_W = RET_HEADS * RET_DK
RET_V_W = RET_HEADS * RET_DV
ROPE_BASE = 10000.0
RET_CHUNK = 256
NA_HEADS = 16
NA_HEAD_DIM = 64
NA_W = NA_HEADS * NA_HEAD_DIM
NA_WIN_ROWS = 8
NA_WIN_COLS = 16
IN_COLS = 2 * RET_QK_W + 2 * RET_V_W + 3 * NA_W + 2 * D_MODEL
N_GROUPS = 4
EXPERTS_PER_GROUP = 4
N_EXPERTS = N_GROUPS * EXPERTS_PER_GROUP
D_FF_EXPERT = 256
LN_EPS = 1e-5
GN_EPS = 1e-5

COL_TILE = 1024
QK_COLS = 2 * RET_QK_W
REST_COLS = IN_COLS - QK_COLS
CB_QR, CB_KR = 0, 1
CB_VR, CB_GR, CB_QN, CB_KN, CB_VN, CB_GATE_R, CB_GATE_N = 0, 2, 4, 5, 6, 7, 8

ROUTER_LANES = 128
POST_SUB_ROWS = 256
MOE_ROW_W = D_MODEL + ROUTER_LANES
MOE_TILE = 512
IN_PROJ_ROWS = 2048
NA_ROWS_PER_STEP = 8
NA_HEADS_PER_GROUP = 4
NA_GROUP_W = NA_HEADS_PER_GROUP * NA_HEAD_DIM
NA_MASK_VALUE = -1e30
V7X_VMEM_BYTES = 64 * 1024 * 1024


def _vmem_limit(estimate_bytes):
    return int(min(V7X_VMEM_BYTES - 8 * 1024 * 1024, max(32 * 1024 * 1024, estimate_bytes * 3 // 2)))


def _sigmoid(x):
    return 1.0 / (1.0 + jnp.exp(-x))


def _layer_norm_rows(y, g, b):
    mu = jnp.mean(y, axis=-1, keepdims=True)
    d = y - mu
    var = jnp.mean(d * d, axis=-1, keepdims=True)
    return d * lax.rsqrt(var + LN_EPS) * g + b


def _in_proj_rotary_kernel(x_ref, w_ref, cos_ref, sin_ref, o_ref, xb_ref):
    @pl.when(pl.program_id(1) == 0)
    def _():
        xb_ref[...] = x_ref[...].astype(BF16)

    acc = jnp.dot(xb_ref[...], w_ref[...], preferred_element_type=F32)
    cos = cos_ref[...]
    sin = sin_ref[...]
    half = RET_DK // 2
    for h in range(RET_HEADS):
        x1 = acc[:, h * RET_DK:h * RET_DK + half]
        x2 = acc[:, h * RET_DK + half:(h + 1) * RET_DK]
        o_ref[:, h * RET_DK:h * RET_DK + half] = (x1 * cos - x2 * sin).astype(BF16)
        o_ref[:, h * RET_DK + half:(h + 1) * RET_DK] = (x1 * sin + x2 * cos).astype(BF16)


def _in_proj_plain_kernel(x_ref, w_ref, o_ref, xb_ref):
    @pl.when(pl.program_id(1) == 0)
    def _():
        xb_ref[...] = x_ref[...].astype(BF16)

    o_ref[...] = jnp.dot(xb_ref[...], w_ref[...], preferred_element_type=F32).astype(BF16)


def _in_proj(x2, w_in, cos, sin, seq_len):
    n = x2.shape[0]
    tm = min(IN_PROJ_ROWS, seq_len)
    tiles_per_seq = seq_len // tm
    qk_blocks = QK_COLS // COL_TILE
    est = (2 * tm * D_MODEL * 4 + tm * D_MODEL * 2 + 2 * D_MODEL * COL_TILE * 2 + 2 * tm * COL_TILE * 2
           + tm * COL_TILE * 4)
    params = pltpu.CompilerParams(dimension_semantics=("arbitrary", "arbitrary"), vmem_limit_bytes=_vmem_limit(est))
    x_spec = pl.BlockSpec((tm, D_MODEL), lambda i, j: (i, 0))
    w_spec = pl.BlockSpec((D_MODEL, COL_TILE), lambda i, j: (0, j))
    o_spec = pl.BlockSpec((tm, COL_TILE), lambda i, j: (i, j))
    rot_spec = pl.BlockSpec((tm, RET_DK // 2), lambda i, j: (i % tiles_per_seq, 0))
    qk = pl.pallas_call(
        _in_proj_rotary_kernel,
        out_shape=jax.ShapeDtypeStruct((n, QK_COLS), BF16),
        grid=(n // tm, QK_COLS // COL_TILE),
        in_specs=[x_spec, w_spec, rot_spec, rot_spec],
        out_specs=o_spec,
        scratch_shapes=[pltpu.VMEM((tm, D_MODEL), BF16)],
        compiler_params=params,
        name="in_proj_qk",
    )(x2, w_in, cos, sin)
    rest = pl.pallas_call(
        _in_proj_plain_kernel,
        out_shape=jax.ShapeDtypeStruct((n, REST_COLS), BF16),
        grid=(n // tm, REST_COLS // COL_TILE),
        in_specs=[x_spec, pl.BlockSpec((D_MODEL, COL_TILE), lambda i, j: (0, j + qk_blocks))],
        out_specs=o_spec,
        scratch_shapes=[pltpu.VMEM((tm, D_MODEL), BF16)],
        compiler_params=params,
        name="in_proj_rest",
    )(x2, w_in)
    return qk, rest


def _ret_tables(lg_ref, chunk, dmat_ref, qd_ref, kd_ref, cd_ref, direction):
    ii = lax.broadcasted_iota(jnp.int32, (chunk, chunk), 0)
    jj = lax.broadcasted_iota(jnp.int32, (chunk, chunk), 1)
    diff = (ii - jj).astype(F32)
    pos_q = lax.broadcasted_iota(jnp.int32, (chunk, RET_DV), 0).astype(F32)
    pos_k = lax.broadcasted_iota(jnp.int32, (chunk, RET_DK), 0).astype(F32)
    for h in range(RET_HEADS):
        lf = lg_ref[0, h]
        lb = lg_ref[1, h]
        if dmat_ref is not None:
            dmat_ref[h] = jnp.where(diff >= 0, jnp.exp(jnp.maximum(diff, 0.0) * lf),
                                    jnp.exp(jnp.maximum(-diff, 0.0) * lb))
        if direction == 0:
            qd_ref[h] = jnp.exp((pos_q + 1.0) * lf)
            kd_ref[h] = jnp.exp((chunk - 1.0 - pos_k) * lf)
            cd_ref[h] = jnp.exp(jnp.zeros((8, RET_DV), F32) + chunk * lf)
        else:
            qd_ref[h] = jnp.exp((chunk - pos_q) * lb)
            kd_ref[h] = jnp.exp(pos_k * lb)
            cd_ref[h] = jnp.exp(jnp.zeros((8, RET_DV), F32) + chunk * lb)


def _ret_cross_and_update(q, k, v, h, state_ref, qd_ref, kd_ref, cd_ref):
    st = state_ref[h]
    cross = jnp.dot(q, st.astype(BF16), preferred_element_type=F32) * qd_ref[h]
    kd = (k.astype(F32) * kd_ref[h]).astype(BF16)
    upd = lax.dot_general(kd, v, (((0,), (0,)), ((), ())), preferred_element_type=F32)
    state_ref[h] = st * cd_ref[h, 0:1, :] + upd
    return cross


def _ret_fwd_kernel(lg_ref, q_ref, k_ref, v_ref, o_ref, state_ref, dmat_ref, qd_ref, kd_ref, cd_ref, *, chunk,
                    n_chunks):
    first = (pl.program_id(0) == 0) & (pl.program_id(1) == 0)

    @pl.when(first)
    def _():
        _ret_tables(lg_ref, chunk, dmat_ref, qd_ref, kd_ref, cd_ref, 0)

    @pl.when(pl.program_id(1) == 0)
    def _():
        state_ref[...] = jnp.zeros(state_ref.shape, F32)

    def body(c, carry):
        rows = pl.ds(pl.multiple_of(c * chunk, chunk), chunk)
        for h in range(RET_HEADS):
            q = q_ref[rows, h * RET_DK:(h + 1) * RET_DK]
            k = k_ref[rows, h * RET_DK:(h + 1) * RET_DK]
            v = v_ref[rows, h * RET_DV:(h + 1) * RET_DV]
            s = lax.dot_general(q, k, (((1,), (1,)), ((), ())), preferred_element_type=F32)
            p = (s * dmat_ref[h]).astype(BF16)
            o = jnp.dot(p, v, preferred_element_type=F32)
            o = o + _ret_cross_and_update(q, k, v, h, state_ref, qd_ref, kd_ref, cd_ref)
            o_ref[rows, h * RET_DV:(h + 1) * RET_DV] = o.astype(o_ref.dtype)
        return carry

    lax.fori_loop(0, n_chunks, body, 0)


def _ret_bwd_kernel(lg_ref, q_ref, k_ref, v_ref, oa_ref, o_ref, state_ref, qd_ref, kd_ref, cd_ref, *, chunk, n_chunks):
    first = (pl.program_id(0) == 0) & (pl.program_id(1) == 0)

    @pl.when(first)
    def _():
        _ret_tables(lg_ref, chunk, None, qd_ref, kd_ref, cd_ref, 1)

    @pl.when(pl.program_id(1) == 0)
    def _():
        state_ref[...] = jnp.zeros(state_ref.shape, F32)

    def body(cc, carry):
        c = n_chunks - 1 - cc
        rows = pl.ds(pl.multiple_of(c * chunk, chunk), chunk)
        for h in range(RET_HEADS):
            q = q_ref[rows, h * RET_DK:(h + 1) * RET_DK]
            k = k_ref[rows, h * RET_DK:(h + 1) * RET_DK]
            v = v_ref[rows, h * RET_DV:(h + 1) * RET_DV]
            o = oa_ref[rows, h * RET_DV:(h + 1) * RET_DV].astype(F32)
            o = o + _ret_cross_and_update(q, k, v, h, state_ref, qd_ref, kd_ref, cd_ref)
            o_ref[rows, h * RET_DV:(h + 1) * RET_DV] = o.astype(o_ref.dtype)
        return carry

    lax.fori_loop(0, n_chunks, body, 0)


def _ret_step_tokens(seq_len, chunk):
    return min(512, seq_len) // chunk * chunk


def _ret_fwd(qk, rest, lg, batch, seq_len, chunk):
    ts = _ret_step_tokens(seq_len, chunk)
    nt = seq_len // ts
    n = batch * seq_len
    est = (2 * (2 * ts * RET_QK_W * 2 + ts * RET_V_W * 2 + ts * RET_V_W * 2)
           + RET_HEADS * (RET_DK * RET_DV * 4 + chunk * chunk * 4 + chunk * RET_DV * 4 + chunk * RET_DK * 4))
    kern = functools.partial(_ret_fwd_kernel, chunk=chunk, n_chunks=ts // chunk)
    return pl.pallas_call(
        kern,
        out_shape=jax.ShapeDtypeStruct((n, RET_V_W), BF16),
        grid=(batch, nt),
        in_specs=[
            pl.BlockSpec(memory_space=pltpu.SMEM),
            pl.BlockSpec((ts, RET_QK_W), lambda b, t: (b * nt + t, CB_QR)),
            pl.BlockSpec((ts, RET_QK_W), lambda b, t: (b * nt + t, CB_KR)),
            pl.BlockSpec((ts, RET_V_W), lambda b, t: (b * nt + t, CB_VR * COL_TILE // RET_V_W)),
        ],
        out_specs=pl.BlockSpec((ts, RET_V_W), lambda b, t: (b * nt + t, 0)),
        scratch_shapes=[
            pltpu.VMEM((RET_HEADS, RET_DK, RET_DV), F32),
            pltpu.VMEM((RET_HEADS, chunk, chunk), F32),
            pltpu.VMEM((RET_HEADS, chunk, RET_DV), F32),
            pltpu.VMEM((RET_HEADS, chunk, RET_DK), F32),
            pltpu.VMEM((RET_HEADS, 8, RET_DV), F32),
        ],
        compiler_params=pltpu.CompilerParams(
            dimension_semantics=("arbitrary", "arbitrary"), vmem_limit_bytes=_vmem_limit(est)),
        name="ret_fwd",
    )(lg, qk, qk, rest)


def _ret_bwd(qk, rest, oa, lg, batch, seq_len, chunk):
    ts = _ret_step_tokens(seq_len, chunk)
    nt = seq_len // ts
    n = batch * seq_len
    est = (2 * (2 * ts * RET_QK_W * 2 + 3 * ts * RET_V_W * 2)
           + RET_HEADS * (RET_DK * RET_DV * 4 + chunk * RET_DV * 4 + chunk * RET_DK * 4))
    kern = functools.partial(_ret_bwd_kernel, chunk=chunk, n_chunks=ts // chunk)

    def rev(b, t):
        return b * nt + (nt - 1 - t)

    return pl.pallas_call(
        kern,
        out_shape=jax.ShapeDtypeStruct((n, RET_V_W), BF16),
        grid=(batch, nt),
        in_specs=[
            pl.BlockSpec(memory_space=pltpu.SMEM),
            pl.BlockSpec((ts, RET_QK_W), lambda b, t: (rev(b, t), CB_QR)),
            pl.BlockSpec((ts, RET_QK_W), lambda b, t: (rev(b, t), CB_KR)),
            pl.BlockSpec((ts, RET_V_W), lambda b, t: (rev(b, t), CB_VR * COL_TILE // RET_V_W)),
            pl.BlockSpec((ts, RET_V_W), lambda b, t: (rev(b, t), 0)),
        ],
        out_specs=pl.BlockSpec((ts, RET_V_W), lambda b, t: (rev(b, t), 0)),
        scratch_shapes=[
            pltpu.VMEM((RET_HEADS, RET_DK, RET_DV), F32),
            pltpu.VMEM((RET_HEADS, chunk, RET_DV), F32),
            pltpu.VMEM((RET_HEADS, chunk, RET_DK), F32),
            pltpu.VMEM((RET_HEADS, 8, RET_DV), F32),
        ],
        compiler_params=pltpu.CompilerParams(
            dimension_semantics=("arbitrary", "arbitrary"), vmem_limit_bytes=_vmem_limit(est)),
        name="ret_bwd",
    )(lg, qk, qk, rest, oa)


def _na_bias_table(rpb):
    cols = np.arange(GRID_W)
    col_start = np.clip(cols - NA_WIN_COLS // 2, 0, GRID_W - NA_WIN_COLS)
    valid = (cols[None, :] >= col_start[:, None]) & (cols[None, :] < col_start[:, None] + NA_WIN_COLS)
    pad = GRID_W
    padded = jnp.pad(rpb.astype(F32), ((0, 0), (0, 0), (pad, pad)))
    tb = jnp.stack([padded[:, :, pad + NA_WIN_COLS - 1 - c:pad + NA_WIN_COLS - 1 - c + GRID_W] for c in range(GRID_W)],
                   axis=1)
    tb = jnp.where(valid[None, :, None, :], tb, NA_MASK_VALUE)
    return jnp.stack([tb[:, :, s:s + NA_WIN_ROWS].reshape(NA_HEADS * GRID_W, NA_WIN_ROWS * GRID_W)
                      for s in range(NA_WIN_ROWS)], 0)


def _na_kernel(q_ref, kp_ref, kc_ref, kn_ref, vp_ref, vc_ref, vn_ref, bias_ref, o_ref, kbuf, vbuf, *, grid_rows):
    t = pl.program_id(1)
    blk = NA_ROWS_PER_STEP * GRID_W
    win = NA_WIN_ROWS * GRID_W
    kbuf[0:blk] = kp_ref[...]
    kbuf[blk:2 * blk] = kc_ref[...]
    kbuf[2 * blk:3 * blk] = kn_ref[...]
    vbuf[0:blk] = vp_ref[...]
    vbuf[blk:2 * blk] = vc_ref[...]
    vbuf[2 * blk:3 * blk] = vn_ref[...]

    rr = lax.broadcasted_iota(jnp.int32, (NA_GROUP_W, NA_GROUP_W), 0) // GRID_W
    ll = lax.broadcasted_iota(jnp.int32, (NA_GROUP_W, NA_GROUP_W), 1) // NA_HEAD_DIM
    head_mask = rr == ll

    def body(i, carry):
        r = t * NA_ROWS_PER_STEP + i
        r0 = jnp.clip(r - NA_WIN_ROWS // 2, 0, grid_rows - NA_WIN_ROWS)
        shift = r0 - r + NA_WIN_ROWS - 1
        off = pl.multiple_of((r0 - (t - 1) * NA_ROWS_PER_STEP) * GRID_W, GRID_W)
        qrows = pl.ds(pl.multiple_of(i * GRID_W, GRID_W), GRID_W)
        groups = [slice(g * NA_GROUP_W, (g + 1) * NA_GROUP_W) for g in range(NA_HEADS // NA_HEADS_PER_GROUP)]
        scores = []
        for lanes in groups:
            q4 = q_ref[qrows, lanes]
            qm = jnp.where(head_mask, jnp.concatenate([q4] * NA_HEADS_PER_GROUP, axis=0), jnp.zeros((), BF16))
            k4 = kbuf[pl.ds(off, win), lanes]
            sc = lax.dot_general(qm, k4, (((1,), (1,)), ((), ())), preferred_element_type=F32)
            scores.append(sc + bias_ref[shift, lanes, :])
        probs = []
        for sc in scores:
            p = jnp.exp(sc - jnp.max(sc, axis=-1, keepdims=True))
            probs.append((p.astype(BF16), jnp.sum(p, axis=-1, keepdims=True)))
        for lanes, (pb, l) in zip(groups, probs):
            v4 = vbuf[pl.ds(off, win), lanes]
            oall = jnp.dot(pb, v4, preferred_element_type=F32) * (1.0 / l)
            oall = jnp.where(head_mask, oall, 0.0)
            o4 = oall[0:GRID_W]
            for hh in range(1, NA_HEADS_PER_GROUP):
                o4 = o4 + oall[hh * GRID_W:(hh + 1) * GRID_W]
            o_ref[qrows, lanes] = o4.astype(o_ref.dtype)
        return carry

    lax.fori_loop(0, NA_ROWS_PER_STEP, body, 0, unroll=2)


def _na(rest, bias_tbl, batch, seq_len):
    grid_rows = seq_len // GRID_W
    assert grid_rows % NA_ROWS_PER_STEP == 0 and grid_rows >= NA_WIN_ROWS
    nb = grid_rows // NA_ROWS_PER_STEP
    blk = NA_ROWS_PER_STEP * GRID_W
    n = batch * seq_len
    est = (2 * 7 * blk * NA_W * 2 + 2 * blk * NA_W * 2 + 2 * 3 * blk * NA_W * 2
           + NA_WIN_ROWS * NA_W * NA_WIN_ROWS * GRID_W * 4)

    def cur(b, t):
        return b * nb + t

    def prev(b, t):
        return b * nb + jnp.maximum(t - 1, 0)

    def nxt(b, t):
        return b * nb + jnp.minimum(t + 1, nb - 1)

    kern = functools.partial(_na_kernel, grid_rows=grid_rows)
    return pl.pallas_call(
        kern,
        out_shape=jax.ShapeDtypeStruct((n, NA_W), BF16),
        grid=(batch, nb),
        in_specs=[
            pl.BlockSpec((blk, NA_W), lambda b, t: (cur(b, t), CB_QN)),
            pl.BlockSpec((blk, NA_W), lambda b, t: (prev(b, t), CB_KN)),
            pl.BlockSpec((blk, NA_W), lambda b, t: (cur(b, t), CB_KN)),
            pl.BlockSpec((blk, NA_W), lambda b, t: (nxt(b, t), CB_KN)),
            pl.BlockSpec((blk, NA_W), lambda b, t: (prev(b, t), CB_VN)),
            pl.BlockSpec((blk, NA_W), lambda b, t: (cur(b, t), CB_VN)),
            pl.BlockSpec((blk, NA_W), lambda b, t: (nxt(b, t), CB_VN)),
            pl.BlockSpec(memory_space=pltpu.VMEM),
        ],
        out_specs=pl.BlockSpec((blk, NA_W), lambda b, t: (cur(b, t), 0)),
        scratch_shapes=[pltpu.VMEM((3 * blk, NA_W), BF16), pltpu.VMEM((3 * blk, NA_W), BF16)],
        compiler_params=pltpu.CompilerParams(
            dimension_semantics=("arbitrary", "arbitrary"), vmem_limit_bytes=_vmem_limit(est)),
        name="na",
    )(rest, rest, rest, rest, rest, rest, rest, bias_tbl)


def _post_kernel(x_ref, o_ref, og_ref, na_ref, gr_ref, gn_ref, wret_ref, wna_ref, wmix_ref, g_ref, b_ref, wr_ref,
                 br_ref, x1e_ref, *, alpha, sub_rows):
    for r0 in range(0, x_ref.shape[0], sub_rows):
        rows = slice(r0, r0 + sub_rows)
        gated = []
        for h in range(RET_HEADS):
            lanes = slice(h * RET_DV, (h + 1) * RET_DV)
            o = o_ref[rows, lanes].astype(F32)
            mu = jnp.mean(o, axis=-1, keepdims=True)
            d = o - mu
            var = jnp.mean(d * d, axis=-1, keepdims=True)
            og = og_ref[rows, lanes].astype(F32)
            gated.append((og * _sigmoid(og) * (d * lax.rsqrt(var + GN_EPS))).astype(BF16))
        ret = jnp.dot(jnp.concatenate(gated, axis=1), wret_ref[...], preferred_element_type=F32)
        na_p = jnp.dot(na_ref[rows, :], wna_ref[...], preferred_element_type=F32)
        merged = _sigmoid(gr_ref[rows, :].astype(F32)) * ret + _sigmoid(gn_ref[rows, :].astype(F32)) * na_p
        y = alpha * x_ref[rows, :] + jnp.dot(merged.astype(BF16), wmix_ref[...], preferred_element_type=F32)
        x1 = _layer_norm_rows(y, g_ref[...], b_ref[...])
        x1e_ref[rows, :D_MODEL] = x1
        logits = jnp.dot(x1.astype(BF16), wr_ref[...], preferred_element_type=F32) + br_ref[...]
        x1e_ref[rows, D_MODEL:] = _route_record(logits)


def _post(x2, o_ret, na, rest, wret, wna, wmix, ln_g, ln_b, wr, br, alpha):
    n = x2.shape[0]
    tm = min(512, n)
    est = (2 * tm * D_MODEL * (4 + 2 + 2 + 2) + 2 * tm * MOE_ROW_W * 4 + 2 * 2 * tm * RET_V_W * 2
           + tm * RET_V_W * 2 + (RET_V_W + 2 * D_MODEL + ROUTER_LANES) * D_MODEL * 2)
    row = lambda i: (i, 0)
    whole = pl.BlockSpec(memory_space=pltpu.VMEM)
    return pl.pallas_call(
        functools.partial(_post_kernel, alpha=alpha, sub_rows=min(POST_SUB_ROWS, tm)),
        out_shape=jax.ShapeDtypeStruct((n, MOE_ROW_W), F32),
        grid=(n // tm,),
        in_specs=[
            pl.BlockSpec((tm, D_MODEL), row),
            pl.BlockSpec((tm, RET_V_W), row),
            pl.BlockSpec((tm, RET_V_W), lambda i: (i, CB_GR * COL_TILE // RET_V_W)),
            pl.BlockSpec((tm, NA_W), row),
            pl.BlockSpec((tm, D_MODEL), lambda i: (i, CB_GATE_R)),
            pl.BlockSpec((tm, D_MODEL), lambda i: (i, CB_GATE_N)),
            whole, whole, whole, whole, whole, whole, whole,
        ],
        out_specs=pl.BlockSpec((tm, MOE_ROW_W), row),
        compiler_params=pltpu.CompilerParams(
            dimension_semantics=("arbitrary",), vmem_limit_bytes=_vmem_limit(est)),
        name="post",
    )(x2, o_ret, rest, na, rest, rest, wret, wna, wmix, ln_g, ln_b, wr, br)


def _route_record(logits):
    lt = logits.T
    gl = [lt[i:i + 1, :] for i in range(N_GROUPS)]
    gmax = functools.reduce(jnp.maximum, gl)
    denom = functools.reduce(lambda a, b: a + b, [jnp.exp(v - gmax) for v in gl])
    group_p = 1.0 / denom
    gidx = jnp.full(gmax.shape, N_GROUPS - 1, jnp.int32)
    for i in range(N_GROUPS - 2, -1, -1):
        gidx = jnp.where(gl[i] == gmax, i, gidx)
    el = []
    for e in range(EXPERTS_PER_GROUP):
        c = N_GROUPS + (N_GROUPS - 1) * EXPERTS_PER_GROUP + e
        v = lt[c:c + 1, :]
        for gi in range(N_GROUPS - 2, -1, -1):
            c = N_GROUPS + gi * EXPERTS_PER_GROUP + e
            v = jnp.where(gidx == gi, lt[c:c + 1, :], v)
        el.append(v)
    m1 = functools.reduce(jnp.maximum, el)
    i1 = jnp.full(m1.shape, EXPERTS_PER_GROUP - 1, jnp.int32)
    for e in range(EXPERTS_PER_GROUP - 2, -1, -1):
        i1 = jnp.where(el[e] == m1, e, i1)
    neg = jnp.float32(-jnp.inf)
    rest = [jnp.where(i1 == e, neg, el[e]) for e in range(EXPERTS_PER_GROUP)]
    m2 = functools.reduce(jnp.maximum, rest)
    i2 = jnp.full(m2.shape, EXPERTS_PER_GROUP - 1, jnp.int32)
    for e in range(EXPERTS_PER_GROUP - 2, -1, -1):
        i2 = jnp.where((rest[e] == m2) & (i1 != e), e, i2)
    e2 = jnp.exp(m2 - m1)
    w1 = group_p / (1.0 + e2)
    w2 = group_p * e2 / (1.0 + e2)
    field = lax.broadcasted_iota(jnp.int32, lt.shape, 0)
    rec = jnp.where(field == 0, gidx.astype(F32), 0.0)
    for e in range(EXPERTS_PER_GROUP):
        gate = jnp.where(i1 == e, w1, 0.0) + jnp.where(i2 == e, w2, 0.0)
        rec = jnp.where(field == 1 + e, gate, rec)
    return rec.T


def _moe_plan(gidx, tm):
    n = gidx.shape[0]
    n_tiles = (n // tm + N_GROUPS + 1) // 2 * 2
    groups = jnp.arange(N_GROUPS, dtype=jnp.int32)
    cnt = jnp.sum((gidx[:, None] == groups[None, :]).astype(jnp.int32), axis=0)
    tiles = (cnt + tm - 1) // tm
    tile_end = jnp.cumsum(tiles)
    tile_start = tile_end - tiles
    seg_start = jnp.cumsum(cnt) - cnt
    order = jnp.argsort(gidx, stable=True).astype(jnp.int32)
    t_ids = jnp.arange(n_tiles, dtype=jnp.int32)
    tgrp = jnp.minimum(jnp.sum((t_ids[:, None] >= tile_end[None, :]).astype(jnp.int32), axis=1), N_GROUPS - 1)
    first_rank = (t_ids - tile_start[tgrp]) * tm
    tcnt = jnp.clip(cnt[tgrp] - first_rank, 0, tm)
    p_local = jnp.arange(tm, dtype=jnp.int32)
    sorted_pos = seg_start[tgrp][:, None] + first_rank[:, None] + p_local[None, :]
    src = jnp.where(p_local[None, :] < tcnt[:, None], order[jnp.clip(sorted_pos, 0, n - 1)], 0)
    return src.reshape(n_tiles, 1, tm), tgrp.astype(jnp.int32), tcnt.astype(jnp.int32)


def _moe_kernel(tgrp_ref, tcnt_ref, src_ref, src_next_ref, x_hbm, w1a_ref, w3a_ref, w2a_ref, w1b_ref, w3b_ref, w2b_ref,
                g_ref, b_ref, out_hbm, xbuf, obuf, pad_sink, gsem, ssem, *, alpha, tm):
    del tgrp_ref
    s = pl.program_id(0)
    ns = pl.num_programs(0)

    def start_gather(idx_ref, which, sl):
        for p in range(tm):
            tok = idx_ref[which, 0, p]
            pltpu.make_async_copy(x_hbm.at[pl.ds(tok, 1), :], xbuf.at[sl, pl.ds(p, 1), :], gsem.at[sl]).start(
                priority=p % 2)

    def wait_gather(sl):
        pltpu.make_async_copy(x_hbm.at[pl.ds(0, tm), :], xbuf.at[sl], gsem.at[sl]).wait()

    def wait_scatter(sl):
        pltpu.make_async_copy(obuf.at[sl], out_hbm.at[pl.ds(0, tm), :], ssem.at[sl]).wait()

    def compute(sl, w1_ref, w3_ref, w2_ref):
        xe = xbuf[sl]
        x = xe[:, :D_MODEL]
        xb = x.astype(BF16)
        hs = []
        for e in range(EXPERTS_PER_GROUP):
            a = jnp.dot(xb, w1_ref[e], preferred_element_type=F32)
            b = jnp.dot(xb, w3_ref[e], preferred_element_type=F32)
            gate = xe[:, D_MODEL + 1 + e:D_MODEL + 2 + e]
            hs.append((a * _sigmoid(a) * b * gate).astype(BF16))
        w2 = w2_ref[...].reshape(EXPERTS_PER_GROUP * D_FF_EXPERT, D_MODEL)
        y = alpha * x + jnp.dot(jnp.concatenate(hs, axis=1), w2, preferred_element_type=F32)
        obuf[sl] = _layer_norm_rows(y, g_ref[...], b_ref[...])

    def start_scatter(which, sl):
        n_real = tcnt_ref[2 * s + which]

        def row_copy(p):
            tok = src_ref[which, 0, p]
            return pltpu.make_async_copy(obuf.at[sl, pl.ds(p, 1), :], out_hbm.at[pl.ds(tok, 1), :], ssem.at[sl])

        def sink_copy(p):
            return pltpu.make_async_copy(obuf.at[sl, pl.ds(p, 1), :], pad_sink.at[sl, pl.ds(p, 1), :], ssem.at[sl])

        @pl.when(n_real == tm)
        def _():
            for p in range(tm):
                row_copy(p).start(priority=p % 2)

        @pl.when(n_real < tm)
        def _():
            def real_body(p, carry):
                row_copy(p).start()
                return carry

            def sink_body(p, carry):
                sink_copy(p).start()
                return carry

            lax.fori_loop(0, n_real, real_body, 0)
            lax.fori_loop(n_real, tm, sink_body, 0)

    @pl.when(s == 0)
    def _():
        start_gather(src_ref, 0, 0)

    wait_gather(0)
    start_gather(src_ref, 1, 1)

    @pl.when(s >= 1)
    def _():
        wait_scatter(0)

    compute(0, w1a_ref, w3a_ref, w2a_ref)
    start_scatter(0, 0)
    wait_gather(1)

    @pl.when(s + 1 < ns)
    def _():
        start_gather(src_next_ref, 0, 0)

    @pl.when(s >= 1)
    def _():
        wait_scatter(1)

    compute(1, w1b_ref, w3b_ref, w2b_ref)
    start_scatter(1, 1)

    @pl.when(s == ns - 1)
    def _():
        wait_scatter(0)
        wait_scatter(1)


def _moe(x1e, w1, w3, w2, ln_g, ln_b, alpha):
    n = x1e.shape[0]
    tm = min(MOE_TILE, n)
    gidx = x1e[:, D_MODEL].astype(jnp.int32)
    src, tgrp, tcnt = _moe_plan(gidx, tm)
    n_tiles = src.shape[0]
    assert n_tiles % 2 == 0
    est = (2 * tm * MOE_ROW_W * 4 + 4 * tm * D_MODEL * 4 + 2 * 2 * 3 * EXPERTS_PER_GROUP * D_MODEL * D_FF_EXPERT * 2
           + tm * 2 * EXPERTS_PER_GROUP * D_FF_EXPERT * 4)
    w_up = (EXPERTS_PER_GROUP, D_MODEL, D_FF_EXPERT)
    w_down = (EXPERTS_PER_GROUP, D_FF_EXPERT, D_MODEL)
    grid_spec = pltpu.PrefetchScalarGridSpec(
        num_scalar_prefetch=2,
        grid=(n_tiles // 2,),
        in_specs=[
            pl.BlockSpec((2, 1, tm), lambda s, tg, tc: (s, 0, 0), memory_space=pltpu.SMEM),
            pl.BlockSpec((1, 1, tm), lambda s, tg, tc: (jnp.minimum(2 * s + 2, n_tiles - 1), 0, 0),
                         memory_space=pltpu.SMEM),
            pl.BlockSpec(memory_space=pl.ANY),
            pl.BlockSpec(w_up, lambda s, tg, tc: (tg[2 * s], 0, 0)),
            pl.BlockSpec(w_up, lambda s, tg, tc: (tg[2 * s], 0, 0)),
            pl.BlockSpec(w_down, lambda s, tg, tc: (tg[2 * s], 0, 0)),
            pl.BlockSpec(w_up, lambda s, tg, tc: (tg[2 * s + 1], 0, 0)),
            pl.BlockSpec(w_up, lambda s, tg, tc: (tg[2 * s + 1], 0, 0)),
            pl.BlockSpec(w_down, lambda s, tg, tc: (tg[2 * s + 1], 0, 0)),
            pl.BlockSpec((1, D_MODEL), lambda s, tg, tc: (0, 0)),
            pl.BlockSpec((1, D_MODEL), lambda s, tg, tc: (0, 0)),
        ],
        out_specs=pl.BlockSpec(memory_space=pl.ANY),
        scratch_shapes=[
            pltpu.VMEM((2, tm, MOE_ROW_W), F32),
            pltpu.VMEM((2, tm, D_MODEL), F32),
            pltpu.VMEM((2, tm, D_MODEL), F32),
            pltpu.SemaphoreType.DMA((2,)),
            pltpu.SemaphoreType.DMA((2,)),
        ],
    )
    return pl.pallas_call(
        functools.partial(_moe_kernel, alpha=alpha, tm=tm),
        out_shape=jax.ShapeDtypeStruct((n, D_MODEL), F32),
        grid_spec=grid_spec,
        compiler_params=pltpu.CompilerParams(
            dimension_semantics=("arbitrary",), vmem_limit_bytes=_vmem_limit(est)),
        name="moe",
    )(tgrp, tcnt, src, src, x1e, w1, w3, w2, w1, w3, w2, ln_g, ln_b)


def _rotary_tables(seq_len):
    half = RET_DK // 2
    inv_freq = ROPE_BASE ** (-jnp.arange(half, dtype=F32) / half)
    ang = jnp.arange(seq_len, dtype=F32)[:, None] * inv_freq[None, :]
    return jnp.cos(ang), jnp.sin(ang)


def _prepare_layer(w_in, decay_fwd, decay_bwd, rpb, w_ret_out, w_na_out, w_mix_out, ln1_g, ln1_b, wg, bg, we, be, w1,
                   w3, w2, ln2_g, ln2_b):
    pad = ROUTER_LANES - N_GROUPS - N_EXPERTS
    wr = jnp.concatenate([wg, we.reshape(D_MODEL, N_EXPERTS), jnp.zeros((D_MODEL, pad), F32)], axis=1)
    br = jnp.concatenate([bg.astype(F32), be.reshape(N_EXPERTS).astype(F32), jnp.zeros((pad,), F32)])
    col_scale = jnp.ones((IN_COLS,), F32)
    col_scale = col_scale.at[RET_QK_W:2 * RET_QK_W].set(RET_DK ** -0.5)
    qn0 = QK_COLS + CB_QN * COL_TILE
    col_scale = col_scale.at[qn0:qn0 + NA_W].set(NA_HEAD_DIM ** -0.5)
    w_in = (w_in.astype(F32) * col_scale).astype(BF16)
    return dict(
        w_in=w_in,
        lg=jnp.stack([jax.nn.log_sigmoid(decay_fwd.astype(F32)), jax.nn.log_sigmoid(decay_bwd.astype(F32))]),
        bias_tbl=_na_bias_table(rpb),
        w_ret=w_ret_out.astype(BF16), w_na=w_na_out.astype(BF16), w_mix=w_mix_out.astype(BF16),
        ln1_g=ln1_g.astype(F32).reshape(1, D_MODEL), ln1_b=ln1_b.astype(F32).reshape(1, D_MODEL),
        wr=wr.astype(BF16), br=br.reshape(1, ROUTER_LANES),
        w1=w1.astype(BF16), w3=w3.astype(BF16), w2=w2.astype(BF16),
        ln2_g=ln2_g.astype(F32).reshape(1, D_MODEL), ln2_b=ln2_b.astype(F32).reshape(1, D_MODEL),
    )


def _encoder_layer(x2, p, batch, seq_len, cos, sin, alpha, chunk):
    qk, rest = _in_proj(x2, p["w_in"], cos, sin, seq_len)
    oa = _ret_fwd(qk, rest, p["lg"], batch, seq_len, chunk)
    o_ret = _ret_bwd(qk, rest, oa, p["lg"], batch, seq_len, chunk)
    na = _na(rest, p["bias_tbl"], batch, seq_len)
    x1e = _post(x2, o_ret, na, rest, p["w_ret"], p["w_na"], p["w_mix"], p["ln1_g"], p["ln1_b"], p["wr"], p["br"],
                alpha)
    return _moe(x1e, p["w1"], p["w3"], p["w2"], p["ln2_g"], p["ln2_b"], alpha)


def _trunk(x, layers, alpha, cos, sin, chunk=RET_CHUNK):
    batch, seq_len, _ = x.shape
    x2 = x.reshape(batch * seq_len, D_MODEL)
    for p in layers:
        x2 = _encoder_layer(x2, p, batch, seq_len, cos, sin, alpha, chunk)
    return x2.reshape(batch, seq_len, D_MODEL)


def kernel(x_prompt, x_sample, w_in, ret_decay_fwd, ret_decay_bwd, na_rel_bias, w_ret_out, w_na_out, w_mix_out, ln1_g,
           ln1_b, router_group_w, router_group_b, router_expert_w, router_expert_b, expert_w1, expert_w3, expert_w2,
           ln2_g, ln2_b):
    depth = w_in.shape[0]
    alpha = (2 * depth) ** 0.25
    per_layer = (w_in, ret_decay_fwd, ret_decay_bwd, na_rel_bias, w_ret_out, w_na_out, w_mix_out, ln1_g, ln1_b,
                 router_group_w, router_group_b, router_expert_w, router_expert_b, expert_w1, expert_w3, expert_w2,
                 ln2_g, ln2_b)
    layers = [_prepare_layer(*[a[l] for a in per_layer]) for l in range(depth)]
    cos, sin = _rotary_tables(max(x_prompt.shape[1], x_sample.shape[1]))
    return (_trunk(x_prompt, layers, alpha, cos, sin), _trunk(x_sample, layers, alpha, cos, sin))
```

```python
import functools

import numpy as np
import jax
import jax.numpy as jnp
from jax import lax
from jax.experimental import pallas as pl
from jax.experimental.pallas import tpu as pltpu

F32 = jnp.float32
BF16 = jnp.bfloat16

D_MODEL = 1024
GRID_W = 64
RET_HEADS = 4
RET_DK = 256
RET_DV = 512
RET_QK_W = RET_HEADS * RET_DK
RET_V_W = RET_HEADS * RET_DV
ROPE_BASE = 10000.0
RET_CHUNK = 256
NA_HEADS = 16
NA_HEAD_DIM = 64
NA_W = NA_HEADS * NA_HEAD_DIM
NA_WIN_ROWS = 8
NA_WIN_COLS = 16
IN_COLS = 2 * RET_QK_W + 2 * RET_V_W + 3 * NA_W + 2 * D_MODEL
N_GROUPS = 4
EXPERTS_PER_GROUP = 4
N_EXPERTS = N_GROUPS * EXPERTS_PER_GROUP
D_FF_EXPERT = 256
LN_EPS = 1e-5
GN_EPS = 1e-5

COL_TILE = 1024
QK_COLS = 2 * RET_QK_W
REST_COLS = IN_COLS - QK_COLS
CB_QR, CB_KR = 0, 1
CB_VR, CB_GR, CB_QN, CB_KN, CB_VN, CB_GATE_R, CB_GATE_N = 0, 2, 4, 5, 6, 7, 8

ROUTER_LANES = 128
POST_SUB_ROWS = 256
MOE_ROW_W = D_MODEL + ROUTER_LANES
MOE_TILE = 512
IN_PROJ_ROWS = 2048
NA_ROWS_PER_STEP = 8
NA_HEADS_PER_GROUP = 4
NA_GROUP_W = NA_HEADS_PER_GROUP * NA_HEAD_DIM
NA_MASK_VALUE = -1e30
V7X_VMEM_BYTES = 64 * 1024 * 1024


def _vmem_limit(estimate_bytes):
    return int(min(V7X_VMEM_BYTES - 8 * 1024 * 1024, max(32 * 1024 * 1024, estimate_bytes * 3 // 2)))


def _sigmoid(x):
    return 1.0 / (1.0 + jnp.exp(-x))


def _layer_norm_rows(y, g, b):
    mu = jnp.mean(y, axis=-1, keepdims=True)
    d = y - mu
    var = jnp.mean(d * d, axis=-1, keepdims=True)
    return d * lax.rsqrt(var + LN_EPS) * g + b


def _in_proj_rotary_kernel(x_ref, w_ref, cos_ref, sin_ref, o_ref, xb_ref):
    @pl.when(pl.program_id(1) == 0)
    def _():
        xb_ref[...] = x_ref[...].astype(BF16)

    acc = jnp.dot(xb_ref[...], w_ref[...], preferred_element_type=F32)
    cos = cos_ref[...]
    sin = sin_ref[...]
    half = RET_DK // 2
    for h in range(RET_HEADS):
        x1 = acc[:, h * RET_DK:h * RET_DK + half]
        x2 = acc[:, h * RET_DK + half:(h + 1) * RET_DK]
        o_ref[:, h * RET_DK:h * RET_DK + half] = (x1 * cos - x2 * sin).astype(BF16)
        o_ref[:, h * RET_DK + half:(h + 1) * RET_DK] = (x1 * sin + x2 * cos).astype(BF16)


def _in_proj_plain_kernel(x_ref, w_ref, o_ref, xb_ref):
    @pl.when(pl.program_id(1) == 0)
    def _():
        xb_ref[...] = x_ref[...].astype(BF16)

    o_ref[...] = jnp.dot(xb_ref[...], w_ref[...], preferred_element_type=F32).astype(BF16)


def _in_proj(x2, w_in, cos, sin, seq_len):
    n = x2.shape[0]
    tm = min(IN_PROJ_ROWS, seq_len)
    tiles_per_seq = seq_len // tm
    qk_blocks = QK_COLS // COL_TILE
    est = (2 * tm * D_MODEL * 4 + tm * D_MODEL * 2 + 2 * D_MODEL * COL_TILE * 2 + 2 * tm * COL_TILE * 2
           + tm * COL_TILE * 4)
    params = pltpu.CompilerParams(dimension_semantics=("arbitrary", "arbitrary"), vmem_limit_bytes=_vmem_limit(est))
    x_spec = pl.BlockSpec((tm, D_MODEL), lambda i, j: (i, 0))
    w_spec = pl.BlockSpec((D_MODEL, COL_TILE), lambda i, j: (0, j))
    o_spec = pl.BlockSpec((tm, COL_TILE), lambda i, j: (i, j))
    rot_spec = pl.BlockSpec((tm, RET_DK // 2), lambda i, j: (i % tiles_per_seq, 0))
    qk = pl.pallas_call(
        _in_proj_rotary_kernel,
        out_shape=jax.ShapeDtypeStruct((n, QK_COLS), BF16),
        grid=(n // tm, QK_COLS // COL_TILE),
        in_specs=[x_spec, w_spec, rot_spec, rot_spec],
        out_specs=o_spec,
        scratch_shapes=[pltpu.VMEM((tm, D_MODEL), BF16)],
        compiler_params=params,
        name="in_proj_qk",
    )(x2, w_in, cos, sin)
    rest = pl.pallas_call(
        _in_proj_plain_kernel,
        out_shape=jax.ShapeDtypeStruct((n, REST_COLS), BF16),
        grid=(n // tm, REST_COLS // COL_TILE),
        in_specs=[x_spec, pl.BlockSpec((D_MODEL, COL_TILE), lambda i, j: (0, j + qk_blocks))],
        out_specs=o_spec,
        scratch_shapes=[pltpu.VMEM((tm, D_MODEL), BF16)],
        compiler_params=params,
        name="in_proj_rest",
    )(x2, w_in)
    return qk, rest


def _ret_tables(lg_ref, chunk, dmat_ref, qd_ref, kd_ref, cd_ref, direction):
    ii = lax.broadcasted_iota(jnp.int32, (chunk, chunk), 0)
    jj = lax.broadcasted_iota(jnp.int32, (chunk, chunk), 1)
    diff = (ii - jj).astype(F32)
    pos_q = lax.broadcasted_iota(jnp.int32, (chunk, RET_DV), 0).astype(F32)
    pos_k = lax.broadcasted_iota(jnp.int32, (chunk, RET_DK), 0).astype(F32)
    for h in range(RET_HEADS):
        lf = lg_ref[0, h]
        lb = lg_ref[1, h]
        if dmat_ref is not None:
            dmat_ref[h] = jnp.where(diff >= 0, jnp.exp(jnp.maximum(diff, 0.0) * lf),
                                    jnp.exp(jnp.maximum(-diff, 0.0) * lb))
        if direction == 0:
            qd_ref[h] = jnp.exp((pos_q + 1.0) * lf)
            kd_ref[h] = jnp.exp((chunk - 1.0 - pos_k) * lf)
            cd_ref[h] = jnp.exp(jnp.zeros((8, RET_DV), F32) + chunk * lf)
        else:
            qd_ref[h] = jnp.exp((chunk - pos_q) * lb)
            kd_ref[h] = jnp.exp(pos_k * lb)
            cd_ref[h] = jnp.exp(jnp.zeros((8, RET_DV), F32) + chunk * lb)


def _ret_cross_and_update(q, k, v, h, state_ref, qd_ref, kd_ref, cd_ref):
    st = state_ref[h]
    cross = jnp.dot(q, st.astype(BF16), preferred_element_type=F32) * qd_ref[h]
    kd = (k.astype(F32) * kd_ref[h]).astype(BF16)
    upd = lax.dot_general(kd, v, (((0,), (0,)), ((), ())), preferred_element_type=F32)
    state_ref[h] = st * cd_ref[h, 0:1, :] + upd
    return cross


def _ret_fwd_kernel(lg_ref, q_ref, k_ref, v_ref, o_ref, state_ref, dmat_ref, qd_ref, kd_ref, cd_ref, *, chunk,
                    n_chunks):
    first = (pl.program_id(0) == 0) & (pl.program_id(1) == 0)

    @pl.when(first)
    def _():
        _ret_tables(lg_ref, chunk, dmat_ref, qd_ref, kd_ref, cd_ref, 0)

    @pl.when(pl.program_id(1) == 0)
    def _():
        state_ref[...] = jnp.zeros(state_ref.shape, F32)

    def body(c, carry):
        rows = pl.ds(pl.multiple_of(c * chunk, chunk), chunk)
        for h in range(RET_HEADS):
            q = q_ref[rows, h * RET_DK:(h + 1) * RET_DK]
            k = k_ref[rows, h * RET_DK:(h + 1) * RET_DK]
            v = v_ref[rows, h * RET_DV:(h + 1) * RET_DV]
            s = lax.dot_general(q, k, (((1,), (1,)), ((), ())), preferred_element_type=F32)
            p = (s * dmat_ref[h]).astype(BF16)
            o = jnp.dot(p, v, preferred_element_type=F32)
            o = o + _ret_cross_and_update(q, k, v, h, state_ref, qd_ref, kd_ref, cd_ref)
            o_ref[rows, h * RET_DV:(h + 1) * RET_DV] = o.astype(o_ref.dtype)
        return carry

    lax.fori_loop(0, n_chunks, body, 0)


def _ret_bwd_kernel(lg_ref, q_ref, k_ref, v_ref, oa_ref, o_ref, state_ref, qd_ref, kd_ref, cd_ref, *, chunk, n_chunks):
    first = (pl.program_id(0) == 0) & (pl.program_id(1) == 0)

    @pl.when(first)
    def _():
        _ret_tables(lg_ref, chunk, None, qd_ref, kd_ref, cd_ref, 1)

    @pl.when(pl.program_id(1) == 0)
    def _():
        state_ref[...] = jnp.zeros(state_ref.shape, F32)

    def body(cc, carry):
        c = n_chunks - 1 - cc
        rows = pl.ds(pl.multiple_of(c * chunk, chunk), chunk)
        for h in range(RET_HEADS):
            q = q_ref[rows, h * RET_DK:(h + 1) * RET_DK]
            k = k_ref[rows, h * RET_DK:(h + 1) * RET_DK]
            v = v_ref[rows, h * RET_DV:(h + 1) * RET_DV]
            o = oa_ref[rows, h * RET_DV:(h + 1) * RET_DV].astype(F32)
            o = o + _ret_cross_and_update(q, k, v, h, state_ref, qd_ref, kd_ref, cd_ref)
            o_ref[rows, h * RET_DV:(h + 1) * RET_DV] = o.astype(o_ref.dtype)
        return carry

    lax.fori_loop(0, n_chunks, body, 0)


def _ret_step_tokens(seq_len, chunk):
    return min(512, seq_len) // chunk * chunk


def _ret_fwd(qk, rest, lg, batch, seq_len, chunk):
    ts = _ret_step_tokens(seq_len, chunk)
    nt = seq_len // ts
    n = batch * seq_len
    est = (2 * (2 * ts * RET_QK_W * 2 + ts * RET_V_W * 2 + ts * RET_V_W * 2)
           + RET_HEADS * (RET_DK * RET_DV * 4 + chunk * chunk * 4 + chunk * RET_DV * 4 + chunk * RET_DK * 4))
    kern = functools.partial(_ret_fwd_kernel, chunk=chunk, n_chunks=ts // chunk)
    return pl.pallas_call(
        kern,
        out_shape=jax.ShapeDtypeStruct((n, RET_V_W), BF16),
        grid=(batch, nt),
        in_specs=[
            pl.BlockSpec(memory_space=pltpu.SMEM),
            pl.BlockSpec((ts, RET_QK_W), lambda b, t: (b * nt + t, CB_QR)),
            pl.BlockSpec((ts, RET_QK_W), lambda b, t: (b * nt + t, CB_KR)),
            pl.BlockSpec((ts, RET_V_W), lambda b, t: (b * nt + t, CB_VR * COL_TILE // RET_V_W)),
        ],
        out_specs=pl.BlockSpec((ts, RET_V_W), lambda b, t: (b * nt + t, 0)),
        scratch_shapes=[
            pltpu.VMEM((RET_HEADS, RET_DK, RET_DV), F32),
            pltpu.VMEM((RET_HEADS, chunk, chunk), F32),
            pltpu.VMEM((RET_HEADS, chunk, RET_DV), F32),
            pltpu.VMEM((RET_HEADS, chunk, RET_DK), F32),
            pltpu.VMEM((RET_HEADS, 8, RET_DV), F32),
        ],
        compiler_params=pltpu.CompilerParams(
            dimension_semantics=("arbitrary", "arbitrary"), vmem_limit_bytes=_vmem_limit(est)),
        name="ret_fwd",
    )(lg, qk, qk, rest)


def _ret_bwd(qk, rest, oa, lg, batch, seq_len, chunk):
    ts = _ret_step_tokens(seq_len, chunk)
    nt = seq_len // ts
    n = batch * seq_len
    est = (2 * (2 * ts * RET_QK_W * 2 + 3 * ts * RET_V_W * 2)
           + RET_HEADS * (RET_DK * RET_DV * 4 + chunk * RET_DV * 4 + chunk * RET_DK * 4))
    kern = functools.partial(_ret_bwd_kernel, chunk=chunk, n_chunks=ts // chunk)

    def rev(b, t):
        return b * nt + (nt - 1 - t)

    return pl.pallas_call(
        kern,
        out_shape=jax.ShapeDtypeStruct((n, RET_V_W), BF16),
        grid=(batch, nt),
        in_specs=[
            pl.BlockSpec(memory_space=pltpu.SMEM),
            pl.BlockSpec((ts, RET_QK_W), lambda b, t: (rev(b, t), CB_QR)),
            pl.BlockSpec((ts, RET_QK_W), lambda b, t: (rev(b, t), CB_KR)),
            pl.BlockSpec((ts, RET_V_W), lambda b, t: (rev(b, t), CB_VR * COL_TILE // RET_V_W)),
            pl.BlockSpec((ts, RET_V_W), lambda b, t: (rev(b, t), 0)),
        ],
        out_specs=pl.BlockSpec((ts, RET_V_W), lambda b, t: (rev(b, t), 0)),
        scratch_shapes=[
            pltpu.VMEM((RET_HEADS, RET_DK, RET_DV), F32),
            pltpu.VMEM((RET_HEADS, chunk, RET_DV), F32),
            pltpu.VMEM((RET_HEADS, chunk, RET_DK), F32),
            pltpu.VMEM((RET_HEADS, 8, RET_DV), F32),
        ],
        compiler_params=pltpu.CompilerParams(
            dimension_semantics=("arbitrary", "arbitrary"), vmem_limit_bytes=_vmem_limit(est)),
        name="ret_bwd",
    )(lg, qk, qk, rest, oa)


def _na_bias_table(rpb):
    cols = np.arange(GRID_W)
    col_start = np.clip(cols - NA_WIN_COLS // 2, 0, GRID_W - NA_WIN_COLS)
    valid = (cols[None, :] >= col_start[:, None]) & (cols[None, :] < col_start[:, None] + NA_WIN_COLS)
    pad = GRID_W
    padded = jnp.pad(rpb.astype(F32), ((0, 0), (0, 0), (pad, pad)))
    tb = jnp.stack([padded[:, :, pad + NA_WIN_COLS - 1 - c:pad + NA_WIN_COLS - 1 - c + GRID_W] for c in range(GRID_W)],
                   axis=1)
    tb = jnp.where(valid[None, :, None, :], tb, NA_MASK_VALUE)
    return jnp.stack([tb[:, :, s:s + NA_WIN_ROWS].reshape(NA_HEADS * GRID_W, NA_WIN_ROWS * GRID_W)
                      for s in range(NA_WIN_ROWS)], 0)


def _na_kernel(q_ref, kp_ref, kc_ref, kn_ref, vp_ref, vc_ref, vn_ref, bias_ref, o_ref, kbuf, vbuf, *, grid_rows):
    t = pl.program_id(1)
    blk = NA_ROWS_PER_STEP * GRID_W
    win = NA_WIN_ROWS * GRID_W
    lo = (NA_ROWS_PER_STEP - NA_WIN_ROWS // 2) * GRID_W
    hi = (NA_WIN_ROWS // 2 - 1) * GRID_W
    for buf, prev_ref, cur_ref, next_ref in ((kbuf, kp_ref, kc_ref, kn_ref), (vbuf, vp_ref, vc_ref, vn_ref)):
        buf[lo:blk] = prev_ref[lo:blk]
        buf[blk:2 * blk] = cur_ref[...]
        buf[2 * blk:2 * blk + hi] = next_ref[0:hi]

    rr = lax.broadcasted_iota(jnp.int32, (NA_GROUP_W, NA_GROUP_W), 0) // GRID_W
    ll = lax.broadcasted_iota(jnp.int32, (NA_GROUP_W, NA_GROUP_W), 1) // NA_HEAD_DIM
    head_mask = rr == ll

    def body(i, carry):
        r = t * NA_ROWS_PER_STEP + i
        r0 = jnp.clip(r - NA_WIN_ROWS // 2, 0, grid_rows - NA_WIN_ROWS)
        shift = r0 - r + NA_WIN_ROWS - 1
        off = pl.multiple_of((r0 - (t - 1) * NA_ROWS_PER_STEP) * GRID_W, GRID_W)
        qrows = pl.ds(pl.multiple_of(i * GRID_W, GRID_W), GRID_W)
        groups = [slice(g * NA_GROUP_W, (g + 1) * NA_GROUP_W) for g in range(NA_HEADS // NA_HEADS_PER_GROUP)]
        scores = []
        for lanes in groups:
            q4 = q_ref[qrows, lanes]
            qm = jnp.where(head_mask, jnp.concatenate([q4] * NA_HEADS_PER_GROUP, axis=0), jnp.zeros((), BF16))
            k4 = kbuf[pl.ds(off, win), lanes]
            sc = lax.dot_general(qm, k4, (((1,), (1,)), ((), ())), preferred_element_type=F32)
            scores.append(sc + bias_ref[shift, lanes, :])
        probs = []
        for sc in scores:
            p = jnp.exp(sc - jnp.max(sc, axis=-1, keepdims=True))
            probs.append((p.astype(BF16), jnp.sum(p, axis=-1, keepdims=True)))
        for lanes, (pb, l) in zip(groups, probs):
            v4 = vbuf[pl.ds(off, win), lanes]
            oall = jnp.dot(pb, v4, preferred_element_type=F32) * (1.0 / l)
            oall = jnp.where(head_mask, oall, 0.0)
            o4 = oall[0:GRID_W]
            for hh in range(1, NA_HEADS_PER_GROUP):
                o4 = o4 + oall[hh * GRID_W:(hh + 1) * GRID_W]
            o_ref[qrows, lanes] = o4.astype(o_ref.dtype)
        return carry

    lax.fori_loop(0, NA_ROWS_PER_STEP, body, 0, unroll=2)


def _na(rest, bias_tbl, batch, seq_len):
    grid_rows = seq_len // GRID_W
    assert grid_rows % NA_ROWS_PER_STEP == 0 and grid_rows >= NA_WIN_ROWS
    nb = grid_rows // NA_ROWS_PER_STEP
    blk = NA_ROWS_PER_STEP * GRID_W
    n = batch * seq_len
    est = (2 * 7 * blk * NA_W * 2 + 2 * blk * NA_W * 2 + 2 * 3 * blk * NA_W * 2
           + NA_WIN_ROWS * NA_W * NA_WIN_ROWS * GRID_W * 4)

    def cur(b, t):
        return b * nb + t

    def prev(b, t):
        return b * nb + jnp.maximum(t - 1, 0)

    def nxt(b, t):
        return b * nb + jnp.minimum(t + 1, nb - 1)

    kern = functools.partial(_na_kernel, grid_rows=grid_rows)
    return pl.pallas_call(
        kern,
        out_shape=jax.ShapeDtypeStruct((n, NA_W), BF16),
        grid=(batch, nb),
        in_specs=[
            pl.BlockSpec((blk, NA_W), lambda b, t: (cur(b, t), CB_QN)),
            pl.BlockSpec((blk, NA_W), lambda b, t: (prev(b, t), CB_KN)),
            pl.BlockSpec((blk, NA_W), lambda b, t: (cur(b, t), CB_KN)),
            pl.BlockSpec((blk, NA_W), lambda b, t: (nxt(b, t), CB_KN)),
            pl.BlockSpec((blk, NA_W), lambda b, t: (prev(b, t), CB_VN)),
            pl.BlockSpec((blk, NA_W), lambda b, t: (cur(b, t), CB_VN)),
            pl.BlockSpec((blk, NA_W), lambda b, t: (nxt(b, t), CB_VN)),
            pl.BlockSpec(memory_space=pltpu.VMEM),
        ],
        out_specs=pl.BlockSpec((blk, NA_W), lambda b, t: (cur(b, t), 0)),
        scratch_shapes=[pltpu.VMEM((3 * blk, NA_W), BF16), pltpu.VMEM((3 * blk, NA_W), BF16)],
        compiler_params=pltpu.CompilerParams(
            dimension_semantics=("arbitrary", "arbitrary"), vmem_limit_bytes=_vmem_limit(est)),
        name="na",
    )(rest, rest, rest, rest, rest, rest, rest, bias_tbl)


def _post_kernel(x_ref, o_ref, og_ref, na_ref, gr_ref, gn_ref, wret_ref, wna_ref, wmix_ref, g_ref, b_ref, wr_ref,
                 br_ref, x1e_ref, *, alpha, sub_rows):
    for r0 in range(0, x_ref.shape[0], sub_rows):
        rows = slice(r0, r0 + sub_rows)
        gated = []
        for h in range(RET_HEADS):
            lanes = slice(h * RET_DV, (h + 1) * RET_DV)
            o = o_ref[rows, lanes].astype(F32)
            mu = jnp.mean(o, axis=-1, keepdims=True)
            d = o - mu
            var = jnp.mean(d * d, axis=-1, keepdims=True)
            og = og_ref[rows, lanes].astype(F32)
            gated.append((og * _sigmoid(og) * (d * lax.rsqrt(var + GN_EPS))).astype(BF16))
        ret = jnp.dot(jnp.concatenate(gated, axis=1), wret_ref[...], preferred_element_type=F32)
        na_p = jnp.dot(na_ref[rows, :], wna_ref[...], preferred_element_type=F32)
        merged = _sigmoid(gr_ref[rows, :].astype(F32)) * ret + _sigmoid(gn_ref[rows, :].astype(F32)) * na_p
        y = alpha * x_ref[rows, :] + jnp.dot(merged.astype(BF16), wmix_ref[...], preferred_element_type=F32)
        x1 = _layer_norm_rows(y, g_ref[...], b_ref[...])
        x1e_ref[rows, :D_MODEL] = x1
        logits = jnp.dot(x1.astype(BF16), wr_ref[...], preferred_element_type=F32) + br_ref[...]
        x1e_ref[rows, D_MODEL:] = _route_record(logits)


def _post(x2, o_ret, na, rest, wret, wna, wmix, ln_g, ln_b, wr, br, alpha):
    n = x2.shape[0]
    tm = min(512, n)
    est = (2 * tm * D_MODEL * (4 + 2 + 2 + 2) + 2 * tm * MOE_ROW_W * 4 + 2 * 2 * tm * RET_V_W * 2
           + tm * RET_V_W * 2 + (RET_V_W + 2 * D_MODEL + ROUTER_LANES) * D_MODEL * 2)
    row = lambda i: (i, 0)
    whole = pl.BlockSpec(memory_space=pltpu.VMEM)
    return pl.pallas_call(
        functools.partial(_post_kernel, alpha=alpha, sub_rows=min(POST_SUB_ROWS, tm)),
        out_shape=jax.ShapeDtypeStruct((n, MOE_ROW_W), F32),
        grid=(n // tm,),
        in_specs=[
            pl.BlockSpec((tm, D_MODEL), row),
            pl.BlockSpec((tm, RET_V_W), row),
            pl.BlockSpec((tm, RET_V_W), lambda i: (i, CB_GR * COL_TILE // RET_V_W)),
            pl.BlockSpec((tm, NA_W), row),
            pl.BlockSpec((tm, D_MODEL), lambda i: (i, CB_GATE_R)),
            pl.BlockSpec((tm, D_MODEL), lambda i: (i, CB_GATE_N)),
            whole, whole, whole, whole, whole, whole, whole,
        ],
        out_specs=pl.BlockSpec((tm, MOE_ROW_W), row),
        compiler_params=pltpu.CompilerParams(
            dimension_semantics=("arbitrary",), vmem_limit_bytes=_vmem_limit(est)),
        name="post",
    )(x2, o_ret, rest, na, rest, rest, wret, wna, wmix, ln_g, ln_b, wr, br)


def _route_record(logits):
    lt = logits.T
    gl = [lt[i:i + 1, :] for i in range(N_GROUPS)]
    gmax = functools.reduce(jnp.maximum, gl)
    denom = functools.reduce(lambda a, b: a + b, [jnp.exp(v - gmax) for v in gl])
    group_p = 1.0 / denom
    gidx = jnp.full(gmax.shape, N_GROUPS - 1, jnp.int32)
    for i in range(N_GROUPS - 2, -1, -1):
        gidx = jnp.where(gl[i] == gmax, i, gidx)
    el = []
    for e in range(EXPERTS_PER_GROUP):
        c = N_GROUPS + (N_GROUPS - 1) * EXPERTS_PER_GROUP + e
        v = lt[c:c + 1, :]
        for gi in range(N_GROUPS - 2, -1, -1):
            c = N_GROUPS + gi * EXPERTS_PER_GROUP + e
            v = jnp.where(gidx == gi, lt[c:c + 1, :], v)
        el.append(v)
    m1 = functools.reduce(jnp.maximum, el)
    i1 = jnp.full(m1.shape, EXPERTS_PER_GROUP - 1, jnp.int32)
    for e in range(EXPERTS_PER_GROUP - 2, -1, -1):
        i1 = jnp.where(el[e] == m1, e, i1)
    neg = jnp.float32(-jnp.inf)
    rest = [jnp.where(i1 == e, neg, el[e]) for e in range(EXPERTS_PER_GROUP)]
    m2 = functools.reduce(jnp.maximum, rest)
    i2 = jnp.full(m2.shape, EXPERTS_PER_GROUP - 1, jnp.int32)
    for e in range(EXPERTS_PER_GROUP - 2, -1, -1):
        i2 = jnp.where((rest[e] == m2) & (i1 != e), e, i2)
    e2 = jnp.exp(m2 - m1)
    w1 = group_p / (1.0 + e2)
    w2 = group_p * e2 / (1.0 + e2)
    field = lax.broadcasted_iota(jnp.int32, lt.shape, 0)
    rec = jnp.where(field == 0, gidx.astype(F32), 0.0)
    for e in range(EXPERTS_PER_GROUP):
        gate = jnp.where(i1 == e, w1, 0.0) + jnp.where(i2 == e, w2, 0.0)
        rec = jnp.where(field == 1 + e, gate, rec)
    return rec.T


def _moe_plan(gidx, tm):
    n = gidx.shape[0]
    n_tiles = (n // tm + N_GROUPS + 1) // 2 * 2
    groups = jnp.arange(N_GROUPS, dtype=jnp.int32)
    cnt = jnp.sum((gidx[:, None] == groups[None, :]).astype(jnp.int32), axis=0)
    tiles = (cnt + tm - 1) // tm
    tile_end = jnp.cumsum(tiles)
    tile_start = tile_end - tiles
    seg_start = jnp.cumsum(cnt) - cnt
    order = jnp.argsort(gidx, stable=True).astype(jnp.int32)
    t_ids = jnp.arange(n_tiles, dtype=jnp.int32)
    tgrp = jnp.minimum(jnp.sum((t_ids[:, None] >= tile_end[None, :]).astype(jnp.int32), axis=1), N_GROUPS - 1)
    first_rank = (t_ids - tile_start[tgrp]) * tm
    tcnt = jnp.clip(cnt[tgrp] - first_rank, 0, tm)
    p_local = jnp.arange(tm, dtype=jnp.int32)
    sorted_pos = seg_start[tgrp][:, None] + first_rank[:, None] + p_local[None, :]
    src = jnp.where(p_local[None, :] < tcnt[:, None], order[jnp.clip(sorted_pos, 0, n - 1)], 0)
    return src.reshape(n_tiles, 1, tm), tgrp.astype(jnp.int32), tcnt.astype(jnp.int32)


def _moe_kernel(tgrp_ref, tcnt_ref, src_ref, src_next_ref, x_hbm, w1a_ref, w3a_ref, w2a_ref, w1b_ref, w3b_ref, w2b_ref,
                g_ref, b_ref, out_hbm, xbuf, obuf, pad_sink, gsem, ssem, *, alpha, tm):
    del tgrp_ref
    s = pl.program_id(0)
    ns = pl.num_programs(0)

    def gather_starts(idx_ref, which, sl):
        def start_row(p):
            tok = idx_ref[which, 0, p]
            pltpu.make_async_copy(x_hbm.at[pl.ds(tok, 1), :], xbuf.at[sl, pl.ds(p, 1), :], gsem.at[sl]).start(
                priority=p % 2)
        return [functools.partial(start_row, p) for p in range(tm)]

    def start_gather(idx_ref, which, sl):
        for thunk in gather_starts(idx_ref, which, sl):
            thunk()

    def wait_gather(sl):
        pltpu.make_async_copy(x_hbm.at[pl.ds(0, tm), :], xbuf.at[sl], gsem.at[sl]).wait()

    def wait_scatter(sl):
        pltpu.make_async_copy(obuf.at[sl], out_hbm.at[pl.ds(0, tm), :], ssem.at[sl]).wait()

    def compute(sl, w1_ref, w3_ref, w2_ref, side_work=()):
        share = -(-len(side_work) // (2 * EXPERTS_PER_GROUP))
        xe = xbuf[sl]
        x = xe[:, :D_MODEL]
        xb = x.astype(BF16)
        hs = []
        for e in range(EXPERTS_PER_GROUP):
            for thunk in side_work[2 * e * share:(2 * e + 1) * share]:
                thunk()
            a = jnp.dot(xb, w1_ref[e], preferred_element_type=F32)
            for thunk in side_work[(2 * e + 1) * share:(2 * e + 2) * share]:
                thunk()
            b = jnp.dot(xb, w3_ref[e], preferred_element_type=F32)
            gate = xe[:, D_MODEL + 1 + e:D_MODEL + 2 + e]
            hs.append((a * _sigmoid(a) * b * gate).astype(BF16))
        w2 = w2_ref[...].reshape(EXPERTS_PER_GROUP * D_FF_EXPERT, D_MODEL)
        y = alpha * x + jnp.dot(jnp.concatenate(hs, axis=1), w2, preferred_element_type=F32)
        obuf[sl] = _layer_norm_rows(y, g_ref[...], b_ref[...])

    def sink_copy(p, sl):
        return pltpu.make_async_copy(obuf.at[sl, pl.ds(p, 1), :], pad_sink.at[sl, pl.ds(p, 1), :], ssem.at[sl])

    def scatter_row_copy(which, sl, p):
        tok = src_ref[which, 0, p]
        return pltpu.make_async_copy(obuf.at[sl, pl.ds(p, 1), :], out_hbm.at[pl.ds(tok, 1), :], ssem.at[sl])

    def scatter_starts(which, sl):
        def start_row(p):
            scatter_row_copy(which, sl, p).start(priority=p % 2)
        return [functools.partial(start_row, p) for p in range(tm)]

    def start_partial_scatter(which, sl):
        n_real = tcnt_ref[2 * s + which]

        def real_body(p, carry):
            scatter_row_copy(which, sl, p).start()
            return carry

        def sink_body(p, carry):
            sink_copy(p, sl).start()
            return carry

        lax.fori_loop(0, n_real, real_body, 0)
        lax.fori_loop(n_real, tm, sink_body, 0)

    @pl.when(s == 0)
    def _():
        start_gather(src_ref, 0, 0)
        obuf[...] = jnp.zeros(obuf.shape, F32)
        for sl in range(2):
            for p in range(tm):
                sink_copy(p, sl).start(priority=p % 2)

    def step(full):
        wait_gather(0)
        wait_scatter(0)
        compute(0, w1a_ref, w3a_ref, w2a_ref, gather_starts(src_ref, 1, 1))
        wait_gather(1)
        wait_scatter(1)
        if full:
            compute(1, w1b_ref, w3b_ref, w2b_ref, scatter_starts(0, 0) + gather_starts(src_next_ref, 0, 0))
            for thunk in scatter_starts(1, 1):
                thunk()
        else:
            start_partial_scatter(0, 0)
            compute(1, w1b_ref, w3b_ref, w2b_ref, gather_starts(src_next_ref, 0, 0))
            start_partial_scatter(1, 1)

    all_full = (tcnt_ref[2 * s] == tm) & (tcnt_ref[2 * s + 1] == tm)

    @pl.when(all_full)
    def _():
        step(True)

    @pl.when(jnp.logical_not(all_full))
    def _():
        step(False)

    @pl.when(s == ns - 1)
    def _():
        wait_gather(0)
        wait_scatter(0)
        wait_scatter(1)


def _moe(x1e, w1, w3, w2, ln_g, ln_b, alpha):
    n = x1e.shape[0]
    tm = min(MOE_TILE, n)
    gidx = x1e[:, D_MODEL].astype(jnp.int32)
    src, tgrp, tcnt = _moe_plan(gidx, tm)
    n_tiles = src.shape[0]
    assert n_tiles % 2 == 0
    est = (2 * tm * MOE_ROW_W * 4 + 4 * tm * D_MODEL * 4 + 2 * 2 * 3 * EXPERTS_PER_GROUP * D_MODEL * D_FF_EXPERT * 2
           + tm * 2 * EXPERTS_PER_GROUP * D_FF_EXPERT * 4)
    w_up = (EXPERTS_PER_GROUP, D_MODEL, D_FF_EXPERT)
    w_down = (EXPERTS_PER_GROUP, D_FF_EXPERT, D_MODEL)
    grid_spec = pltpu.PrefetchScalarGridSpec(
        num_scalar_prefetch=2,
        grid=(n_tiles // 2,),
        in_specs=[
            pl.BlockSpec((2, 1, tm), lambda s, tg, tc: (s, 0, 0), memory_space=pltpu.SMEM),
            pl.BlockSpec((1, 1, tm), lambda s, tg, tc: (jnp.minimum(2 * s + 2, n_tiles - 1), 0, 0),
                         memory_space=pltpu.SMEM),
            pl.BlockSpec(memory_space=pl.ANY),
            pl.BlockSpec(w_up, lambda s, tg, tc: (tg[2 * s], 0, 0)),
            pl.BlockSpec(w_up, lambda s, tg, tc: (tg[2 * s], 0, 0)),
            pl.BlockSpec(w_down, lambda s, tg, tc: (tg[2 * s], 0, 0)),
            pl.BlockSpec(w_up, lambda s, tg, tc: (tg[2 * s + 1], 0, 0)),
            pl.BlockSpec(w_up, lambda s, tg, tc: (tg[2 * s + 1], 0, 0)),
            pl.BlockSpec(w_down, lambda s, tg, tc: (tg[2 * s + 1], 0, 0)),
            pl.BlockSpec((1, D_MODEL), lambda s, tg, tc: (0, 0)),
            pl.BlockSpec((1, D_MODEL), lambda s, tg, tc: (0, 0)),
        ],
        out_specs=pl.BlockSpec(memory_space=pl.ANY),
        scratch_shapes=[
            pltpu.VMEM((2, tm, MOE_ROW_W), F32),
            pltpu.VMEM((2, tm, D_MODEL), F32),
            pltpu.VMEM((2, tm, D_MODEL), F32),
            pltpu.SemaphoreType.DMA((2,)),
            pltpu.SemaphoreType.DMA((2,)),
        ],
    )
    return pl.pallas_call(
        functools.partial(_moe_kernel, alpha=alpha, tm=tm),
        out_shape=jax.ShapeDtypeStruct((n, D_MODEL), F32),
        grid_spec=grid_spec,
        compiler_params=pltpu.CompilerParams(
            dimension_semantics=("arbitrary",), vmem_limit_bytes=_vmem_limit(est)),
        name="moe",
    )(tgrp, tcnt, src, src, x1e, w1, w3, w2, w1, w3, w2, ln_g, ln_b)


def _rotary_tables(seq_len):
    half = RET_DK // 2
    inv_freq = ROPE_BASE ** (-jnp.arange(half, dtype=F32) / half)
    ang = jnp.arange(seq_len, dtype=F32)[:, None] * inv_freq[None, :]
    return jnp.cos(ang), jnp.sin(ang)


def _prepare_layer(w_in, decay_fwd, decay_bwd, rpb, w_ret_out, w_na_out, w_mix_out, ln1_g, ln1_b, wg, bg, we, be, w1,
                   w3, w2, ln2_g, ln2_b):
    pad = ROUTER_LANES - N_GROUPS - N_EXPERTS
    wr = jnp.concatenate([wg, we.reshape(D_MODEL, N_EXPERTS), jnp.zeros((D_MODEL, pad), F32)], axis=1)
    br = jnp.concatenate([bg.astype(F32), be.reshape(N_EXPERTS).astype(F32), jnp.zeros((pad,), F32)])
    col_scale = jnp.ones((IN_COLS,), F32)
    col_scale = col_scale.at[RET_QK_W:2 * RET_QK_W].set(RET_DK ** -0.5)
    qn0 = QK_COLS + CB_QN * COL_TILE
    col_scale = col_scale.at[qn0:qn0 + NA_W].set(NA_HEAD_DIM ** -0.5)
    w_in = (w_in.astype(F32) * col_scale).astype(BF16)
    return dict(
        w_in=w_in,
        lg=jnp.stack([jax.nn.log_sigmoid(decay_fwd.astype(F32)), jax.nn.log_sigmoid(decay_bwd.astype(F32))]),
        bias_tbl=_na_bias_table(rpb),
        w_ret=w_ret_out.astype(BF16), w_na=w_na_out.astype(BF16), w_mix=w_mix_out.astype(BF16),
        ln1_g=ln1_g.astype(F32).reshape(1, D_MODEL), ln1_b=ln1_b.astype(F32).reshape(1, D_MODEL),
        wr=wr.astype(BF16), br=br.reshape(1, ROUTER_LANES),
        w1=w1.astype(BF16), w3=w3.astype(BF16), w2=w2.astype(BF16),
        ln2_g=ln2_g.astype(F32).reshape(1, D_MODEL), ln2_b=ln2_b.astype(F32).reshape(1, D_MODEL),
    )


def _encoder_layer(x2, p, batch, seq_len, cos, sin, alpha, chunk):
    qk, rest = _in_proj(x2, p["w_in"], cos, sin, seq_len)
    oa = _ret_fwd(qk, rest, p["lg"], batch, seq_len, chunk)
    o_ret = _ret_bwd(qk, rest, oa, p["lg"], batch, seq_len, chunk)
    na = _na(rest, p["bias_tbl"], batch, seq_len)
    x1e = _post(x2, o_ret, na, rest, p["w_ret"], p["w_na"], p["w_mix"], p["ln1_g"], p["ln1_b"], p["wr"], p["br"],
                alpha)
    return _moe(x1e, p["w1"], p["w3"], p["w2"], p["ln2_g"], p["ln2_b"], alpha)


def _trunk(x, layers, alpha, cos, sin, chunk=RET_CHUNK):
    batch, seq_len, _ = x.shape
    x2 = x.reshape(batch * seq_len, D_MODEL)
    for p in layers:
        x2 = _encoder_layer(x2, p, batch, seq_len, cos, sin, alpha, chunk)
    return x2.reshape(batch, seq_len, D_MODEL)


def kernel(x_prompt, x_sample, w_in, ret_decay_fwd, ret_decay_bwd, na_rel_bias, w_ret_out, w_na_out, w_mix_out, ln1_g,
           ln1_b, router_group_w, router_group_b, router_expert_w, router_expert_b, expert_w1, expert_w3, expert_w2,
           ln2_g, ln2_b):
    depth = w_in.shape[0]
    alpha = (2 * depth) ** 0.25
    per_layer = (w_in, ret_decay_fwd, ret_decay_bwd, na_rel_bias, w_ret_out, w_na_out, w_mix_out, ln1_g, ln1_b,
                 router_group_w, router_group_b, router_expert_w, router_expert_b, expert_w1, expert_w3, expert_w2,
                 ln2_g, ln2_b)
    layers = [_prepare_layer(*[a[l] for a in per_layer]) for l in range(depth)]
    cos, sin = _rotary_tables(max(x_prompt.shape[1], x_sample.shape[1]))
    return (_trunk(x_prompt, layers, alpha, cos, sin), _trunk(x_sample, layers, alpha, cos, sin))
```

```python
import functools

import numpy as np
import jax
import jax.numpy as jnp
from jax import lax
from jax.experimental import pallas as pl
from jax.experimental.pallas import tpu as pltpu

F32 = jnp.float32
BF16 = jnp.bfloat16

D_MODEL = 1024
GRID_W = 64
RET_HEADS = 4
RET_DK = 256
RET_DV = 512
RET_QK_W = RET_HEADS * RET_DK
RET_V_W = RET_HEADS * RET_DV
ROPE_BASE = 10000.0
RET_CHUNK = 256
NA_HEADS = 16
NA_HEAD_DIM = 64
NA_W = NA_HEADS * NA_HEAD_DIM
NA_WIN_ROWS = 8
NA_WIN_COLS = 16
IN_COLS = 2 * RET_QK_W + 2 * RET_V_W + 3 * NA_W + 2 * D_MODEL
N_GROUPS = 4
EXPERTS_PER_GROUP = 4
N_EXPERTS = N_GROUPS * EXPERTS_PER_GROUP
D_FF_EXPERT = 256
LN_EPS = 1e-5
GN_EPS = 1e-5

COL_TILE = 1024
QK_COLS = 2 * RET_QK_W
REST_COLS = IN_COLS - QK_COLS
CB_QR, CB_KR = 0, 1
CB_VR, CB_GR, CB_QN, CB_KN, CB_VN, CB_GATE_R, CB_GATE_N = 0, 2, 4, 5, 6, 7, 8

ROUTER_LANES = 128
POST_SUB_ROWS = 256
MOE_ROW_W = D_MODEL + ROUTER_LANES
MOE_TILE = 512
IN_PROJ_ROWS = 2048
NA_ROWS_PER_STEP = 8
NA_HEADS_PER_GROUP = 4
NA_GROUP_W = NA_HEADS_PER_GROUP * NA_HEAD_DIM
NA_MASK_VALUE = -1e30
V7X_VMEM_BYTES = 64 * 1024 * 1024


def _vmem_limit(estimate_bytes):
    return int(min(V7X_VMEM_BYTES - 8 * 1024 * 1024, max(32 * 1024 * 1024, estimate_bytes * 3 // 2)))


def _sigmoid(x):
    return 1.0 / (1.0 + jnp.exp(-x))


def _layer_norm_rows(y, g, b):
    mu = jnp.mean(y, axis=-1, keepdims=True)
    d = y - mu
    var = jnp.mean(d * d, axis=-1, keepdims=True)
    return d * lax.rsqrt(var + LN_EPS) * g + b


def _in_proj_rotary_kernel(x_ref, w_ref, cos_ref, sin_ref, o_ref, xb_ref):
    @pl.when(pl.program_id(1) == 0)
    def _():
        xb_ref[...] = x_ref[...].astype(BF16)

    acc = jnp.dot(xb_ref[...], w_ref[...], preferred_element_type=F32)
    cos = cos_ref[...]
    sin = sin_ref[...]
    half = RET_DK // 2
    for h in range(RET_HEADS):
        x1 = acc[:, h * RET_DK:h * RET_DK + half]
        x2 = acc[:, h * RET_DK + half:(h + 1) * RET_DK]
        o_ref[:, h * RET_DK:h * RET_DK + half] = (x1 * cos - x2 * sin).astype(BF16)
        o_ref[:, h * RET_DK + half:(h + 1) * RET_DK] = (x1 * sin + x2 * cos).astype(BF16)


def _in_proj_plain_kernel(x_ref, w_ref, o_ref, xb_ref):
    @pl.when(pl.program_id(1) == 0)
    def _():
        xb_ref[...] = x_ref[...].astype(BF16)

    o_ref[...] = jnp.dot(xb_ref[...], w_ref[...], preferred_element_type=F32).astype(BF16)


def _in_proj(x2, w_in, cos, sin, seq_len):
    n = x2.shape[0]
    tm = min(IN_PROJ_ROWS, seq_len)
    tiles_per_seq = seq_len // tm
    qk_blocks = QK_COLS // COL_TILE
    est = (2 * tm * D_MODEL * 4 + tm * D_MODEL * 2 + 2 * D_MODEL * COL_TILE * 2 + 2 * tm * COL_TILE * 2
           + tm * COL_TILE * 4)
    params = pltpu.CompilerParams(dimension_semantics=("arbitrary", "arbitrary"), vmem_limit_bytes=_vmem_limit(est))
    x_spec = pl.BlockSpec((tm, D_MODEL), lambda i, j: (i, 0))
    w_spec = pl.BlockSpec((D_MODEL, COL_TILE), lambda i, j: (0, j))
    o_spec = pl.BlockSpec((tm, COL_TILE), lambda i, j: (i, j))
    rot_spec = pl.BlockSpec((tm, RET_DK // 2), lambda i, j: (i % tiles_per_seq, 0))
    qk = pl.pallas_call(
        _in_proj_rotary_kernel,
        out_shape=jax.ShapeDtypeStruct((n, QK_COLS), BF16),
        grid=(n // tm, QK_COLS // COL_TILE),
        in_specs=[x_spec, w_spec, rot_spec, rot_spec],
        out_specs=o_spec,
        scratch_shapes=[pltpu.VMEM((tm, D_MODEL), BF16)],
        compiler_params=params,
        name="in_proj_qk",
    )(x2, w_in, cos, sin)
    rest = pl.pallas_call(
        _in_proj_plain_kernel,
        out_shape=jax.ShapeDtypeStruct((n, REST_COLS), BF16),
        grid=(n // tm, REST_COLS // COL_TILE),
        in_specs=[x_spec, pl.BlockSpec((D_MODEL, COL_TILE), lambda i, j: (0, j + qk_blocks))],
        out_specs=o_spec,
        scratch_shapes=[pltpu.VMEM((tm, D_MODEL), BF16)],
        compiler_params=params,
        name="in_proj_rest",
    )(x2, w_in)
    return qk, rest


def _ret_tables(lg_ref, chunk, dmat_ref, qd_ref, kd_ref, cd_ref, direction):
    ii = lax.broadcasted_iota(jnp.int32, (chunk, chunk), 0)
    jj = lax.broadcasted_iota(jnp.int32, (chunk, chunk), 1)
    diff = (ii - jj).astype(F32)
    pos_q = lax.broadcasted_iota(jnp.int32, (chunk, RET_DV), 0).astype(F32)
    pos_k = lax.broadcasted_iota(jnp.int32, (chunk, RET_DK), 0).astype(F32)
    for h in range(RET_HEADS):
        lf = lg_ref[0, h]
        lb = lg_ref[1, h]
        if dmat_ref is not None:
            dmat_ref[h] = jnp.where(diff >= 0, jnp.exp(jnp.maximum(diff, 0.0) * lf),
                                    jnp.exp(jnp.maximum(-diff, 0.0) * lb))
        if direction == 0:
            qd_ref[h] = jnp.exp((pos_q + 1.0) * lf)
            kd_ref[h] = jnp.exp((chunk - 1.0 - pos_k) * lf)
            cd_ref[h] = jnp.exp(jnp.zeros((8, RET_DV), F32) + chunk * lf)
        else:
            qd_ref[h] = jnp.exp((chunk - pos_q) * lb)
            kd_ref[h] = jnp.exp(pos_k * lb)
            cd_ref[h] = jnp.exp(jnp.zeros((8, RET_DV), F32) + chunk * lb)


def _ret_cross_and_update(q, k, v, h, state_ref, qd_ref, kd_ref, cd_ref):
    st = state_ref[h]
    cross = jnp.dot(q, st.astype(BF16), preferred_element_type=F32) * qd_ref[h]
    kd = (k.astype(F32) * kd_ref[h]).astype(BF16)
    upd = lax.dot_general(kd, v, (((0,), (0,)), ((), ())), preferred_element_type=F32)
    state_ref[h] = st * cd_ref[h, 0:1, :] + upd
    return cross


def _ret_fwd_kernel(lg_ref, q_ref, k_ref, v_ref, o_ref, state_ref, dmat_ref, qd_ref, kd_ref, cd_ref, *, chunk,
                    n_chunks):
    first = (pl.program_id(0) == 0) & (pl.program_id(1) == 0)

    @pl.when(first)
    def _():
        _ret_tables(lg_ref, chunk, dmat_ref, qd_ref, kd_ref, cd_ref, 0)

    @pl.when(pl.program_id(1) == 0)
    def _():
        state_ref[...] = jnp.zeros(state_ref.shape, F32)

    def body(c, carry):
        rows = pl.ds(pl.multiple_of(c * chunk, chunk), chunk)
        for h in range(RET_HEADS):
            q = q_ref[rows, h * RET_DK:(h + 1) * RET_DK]
            k = k_ref[rows, h * RET_DK:(h + 1) * RET_DK]
            v = v_ref[rows, h * RET_DV:(h + 1) * RET_DV]
            s = lax.dot_general(q, k, (((1,), (1,)), ((), ())), preferred_element_type=F32)
            p = (s * dmat_ref[h]).astype(BF16)
            o = jnp.dot(p, v, preferred_element_type=F32)
            o = o + _ret_cross_and_update(q, k, v, h, state_ref, qd_ref, kd_ref, cd_ref)
            o_ref[rows, h * RET_DV:(h + 1) * RET_DV] = o.astype(o_ref.dtype)
        return carry

    lax.fori_loop(0, n_chunks, body, 0)


def _ret_bwd_kernel(lg_ref, q_ref, k_ref, v_ref, oa_ref, o_ref, state_ref, qd_ref, kd_ref, cd_ref, *, chunk, n_chunks):
    first = (pl.program_id(0) == 0) & (pl.program_id(1) == 0)

    @pl.when(first)
    def _():
        _ret_tables(lg_ref, chunk, None, qd_ref, kd_ref, cd_ref, 1)

    @pl.when(pl.program_id(1) == 0)
    def _():
        state_ref[...] = jnp.zeros(state_ref.shape, F32)

    def body(cc, carry):
        c = n_chunks - 1 - cc
        rows = pl.ds(pl.multiple_of(c * chunk, chunk), chunk)
        for h in range(RET_HEADS):
            q = q_ref[rows, h * RET_DK:(h + 1) * RET_DK]
            k = k_ref[rows, h * RET_DK:(h + 1) * RET_DK]
            v = v_ref[rows, h * RET_DV:(h + 1) * RET_DV]
            o = oa_ref[rows, h * RET_DV:(h + 1) * RET_DV].astype(F32)
            o = o + _ret_cross_and_update(q, k, v, h, state_ref, qd_ref, kd_ref, cd_ref)
            o_ref[rows, h * RET_DV:(h + 1) * RET_DV] = o.astype(o_ref.dtype)
        return carry

    lax.fori_loop(0, n_chunks, body, 0)


def _ret_step_tokens(seq_len, chunk):
    return min(512, seq_len) // chunk * chunk


def _ret_fwd(qk, rest, lg, batch, seq_len, chunk):
    ts = _ret_step_tokens(seq_len, chunk)
    nt = seq_len // ts
    n = batch * seq_len
    est = (2 * (2 * ts * RET_QK_W * 2 + ts * RET_V_W * 2 + ts * RET_V_W * 2)
           + RET_HEADS * (RET_DK * RET_DV * 4 + chunk * chunk * 4 + chunk * RET_DV * 4 + chunk * RET_DK * 4))
    kern = functools.partial(_ret_fwd_kernel, chunk=chunk, n_chunks=ts // chunk)
    return pl.pallas_call(
        kern,
        out_shape=jax.ShapeDtypeStruct((n, RET_V_W), BF16),
        grid=(batch, nt),
        in_specs=[
            pl.BlockSpec(memory_space=pltpu.SMEM),
            pl.BlockSpec((ts, RET_QK_W), lambda b, t: (b * nt + t, CB_QR)),
            pl.BlockSpec((ts, RET_QK_W), lambda b, t: (b * nt + t, CB_KR)),
            pl.BlockSpec((ts, RET_V_W), lambda b, t: (b * nt + t, CB_VR * COL_TILE // RET_V_W)),
        ],
        out_specs=pl.BlockSpec((ts, RET_V_W), lambda b, t: (b * nt + t, 0)),
        scratch_shapes=[
            pltpu.VMEM((RET_HEADS, RET_DK, RET_DV), F32),
            pltpu.VMEM((RET_HEADS, chunk, chunk), F32),
            pltpu.VMEM((RET_HEADS, chunk, RET_DV), F32),
            pltpu.VMEM((RET_HEADS, chunk, RET_DK), F32),
            pltpu.VMEM((RET_HEADS, 8, RET_DV), F32),
        ],
        compiler_params=pltpu.CompilerParams(
            dimension_semantics=("arbitrary", "arbitrary"), vmem_limit_bytes=_vmem_limit(est)),
        name="ret_fwd",
    )(lg, qk, qk, rest)


def _ret_bwd(qk, rest, oa, lg, batch, seq_len, chunk):
    ts = _ret_step_tokens(seq_len, chunk)
    nt = seq_len // ts
    n = batch * seq_len
    est = (2 * (2 * ts * RET_QK_W * 2 + 3 * ts * RET_V_W * 2)
           + RET_HEADS * (RET_DK * RET_DV * 4 + chunk * RET_DV * 4 + chunk * RET_DK * 4))
    kern = functools.partial(_ret_bwd_kernel, chunk=chunk, n_chunks=ts // chunk)

    def rev(b, t):
        return b * nt + (nt - 1 - t)

    return pl.pallas_call(
        kern,
        out_shape=jax.ShapeDtypeStruct((n, RET_V_W), BF16),
        grid=(batch, nt),
        in_specs=[
            pl.BlockSpec(memory_space=pltpu.SMEM),
            pl.BlockSpec((ts, RET_QK_W), lambda b, t: (rev(b, t), CB_QR)),
            pl.BlockSpec((ts, RET_QK_W), lambda b, t: (rev(b, t), CB_KR)),
            pl.BlockSpec((ts, RET_V_W), lambda b, t: (rev(b, t), CB_VR * COL_TILE // RET_V_W)),
            pl.BlockSpec((ts, RET_V_W), lambda b, t: (rev(b, t), 0)),
        ],
        out_specs=pl.BlockSpec((ts, RET_V_W), lambda b, t: (rev(b, t), 0)),
        scratch_shapes=[
            pltpu.VMEM((RET_HEADS, RET_DK, RET_DV), F32),
            pltpu.VMEM((RET_HEADS, chunk, RET_DV), F32),
            pltpu.VMEM((RET_HEADS, chunk, RET_DK), F32),
            pltpu.VMEM((RET_HEADS, 8, RET_DV), F32),
        ],
        compiler_params=pltpu.CompilerParams(
            dimension_semantics=("arbitrary", "arbitrary"), vmem_limit_bytes=_vmem_limit(est)),
        name="ret_bwd",
    )(lg, qk, qk, rest, oa)


def _na_bias_table(rpb):
    cols = np.arange(GRID_W)
    col_start = np.clip(cols - NA_WIN_COLS // 2, 0, GRID_W - NA_WIN_COLS)
    valid = (cols[None, :] >= col_start[:, None]) & (cols[None, :] < col_start[:, None] + NA_WIN_COLS)
    pad = GRID_W
    padded = jnp.pad(rpb.astype(F32), ((0, 0), (0, 0), (pad, pad)))
    tb = jnp.stack([padded[:, :, pad + NA_WIN_COLS - 1 - c:pad + NA_WIN_COLS - 1 - c + GRID_W] for c in range(GRID_W)],
                   axis=1)
    tb = jnp.where(valid[None, :, None, :], tb, NA_MASK_VALUE)
    return jnp.stack([tb[:, :, s:s + NA_WIN_ROWS].reshape(NA_HEADS * GRID_W, NA_WIN_ROWS * GRID_W)
                      for s in range(NA_WIN_ROWS)], 0)


def _na_kernel(q_ref, kp_ref, kc_ref, kn_ref, vp_ref, vc_ref, vn_ref, bias_ref, o_ref, kbuf, vbuf, *, grid_rows):
    t = pl.program_id(1)
    blk = NA_ROWS_PER_STEP * GRID_W
    win = NA_WIN_ROWS * GRID_W
    lo = (NA_ROWS_PER_STEP - NA_WIN_ROWS // 2) * GRID_W
    hi = (NA_WIN_ROWS // 2 - 1) * GRID_W
    for buf, prev_ref, cur_ref, next_ref in ((kbuf, kp_ref, kc_ref, kn_ref), (vbuf, vp_ref, vc_ref, vn_ref)):
        buf[lo:blk] = prev_ref[lo:blk]
        buf[blk:2 * blk] = cur_ref[...]
        buf[2 * blk:2 * blk + hi] = next_ref[0:hi]

    rr = lax.broadcasted_iota(jnp.int32, (NA_GROUP_W, NA_GROUP_W), 0) // GRID_W
    ll = lax.broadcasted_iota(jnp.int32, (NA_GROUP_W, NA_GROUP_W), 1) // NA_HEAD_DIM
    head_mask = rr == ll

    def body(i, carry):
        r = t * NA_ROWS_PER_STEP + i
        r0 = jnp.clip(r - NA_WIN_ROWS // 2, 0, grid_rows - NA_WIN_ROWS)
        shift = r0 - r + NA_WIN_ROWS - 1
        off = pl.multiple_of((r0 - (t - 1) * NA_ROWS_PER_STEP) * GRID_W, GRID_W)
        qrows = pl.ds(pl.multiple_of(i * GRID_W, GRID_W), GRID_W)
        groups = [slice(g * NA_GROUP_W, (g + 1) * NA_GROUP_W) for g in range(NA_HEADS // NA_HEADS_PER_GROUP)]
        scores = []
        for lanes in groups:
            q4 = q_ref[qrows, lanes]
            qm = jnp.where(head_mask, jnp.concatenate([q4] * NA_HEADS_PER_GROUP, axis=0), jnp.zeros((), BF16))
            k4 = kbuf[pl.ds(off, win), lanes]
            sc = lax.dot_general(qm, k4, (((1,), (1,)), ((), ())), preferred_element_type=F32)
            scores.append(sc + bias_ref[shift, lanes, :])
        probs = []
        for sc in scores:
            p = jnp.exp(sc - jnp.max(sc, axis=-1, keepdims=True))
            probs.append((p.astype(BF16), jnp.sum(p, axis=-1, keepdims=True)))
        for lanes, (pb, l) in zip(groups, probs):
            v4 = vbuf[pl.ds(off, win), lanes]
            oall = jnp.dot(pb, v4, preferred_element_type=F32) * (1.0 / l)
            oall = jnp.where(head_mask, oall, 0.0)
            o4 = oall[0:GRID_W]
            for hh in range(1, NA_HEADS_PER_GROUP):
                o4 = o4 + oall[hh * GRID_W:(hh + 1) * GRID_W]
            o_ref[qrows, lanes] = o4.astype(o_ref.dtype)
        return carry

    lax.fori_loop(0, NA_ROWS_PER_STEP, body, 0, unroll=4)


def _na(rest, bias_tbl, batch, seq_len):
    grid_rows = seq_len // GRID_W
    assert grid_rows % NA_ROWS_PER_STEP == 0 and grid_rows >= NA_WIN_ROWS
    nb = grid_rows // NA_ROWS_PER_STEP
    blk = NA_ROWS_PER_STEP * GRID_W
    n = batch * seq_len
    est = (2 * 7 * blk * NA_W * 2 + 2 * blk * NA_W * 2 + 2 * 3 * blk * NA_W * 2
           + NA_WIN_ROWS * NA_W * NA_WIN_ROWS * GRID_W * 4)

    def cur(b, t):
        return b * nb + t

    def prev(b, t):
        return b * nb + jnp.maximum(t - 1, 0)

    def nxt(b, t):
        return b * nb + jnp.minimum(t + 1, nb - 1)

    kern = functools.partial(_na_kernel, grid_rows=grid_rows)
    return pl.pallas_call(
        kern,
        out_shape=jax.ShapeDtypeStruct((n, NA_W), BF16),
        grid=(batch, nb),
        in_specs=[
            pl.BlockSpec((blk, NA_W), lambda b, t: (cur(b, t), CB_QN)),
            pl.BlockSpec((blk, NA_W), lambda b, t: (prev(b, t), CB_KN)),
            pl.BlockSpec((blk, NA_W), lambda b, t: (cur(b, t), CB_KN)),
            pl.BlockSpec((blk, NA_W), lambda b, t: (nxt(b, t), CB_KN)),
            pl.BlockSpec((blk, NA_W), lambda b, t: (prev(b, t), CB_VN)),
            pl.BlockSpec((blk, NA_W), lambda b, t: (cur(b, t), CB_VN)),
            pl.BlockSpec((blk, NA_W), lambda b, t: (nxt(b, t), CB_VN)),
            pl.BlockSpec(memory_space=pltpu.VMEM),
        ],
        out_specs=pl.BlockSpec((blk, NA_W), lambda b, t: (cur(b, t), 0)),
        scratch_shapes=[pltpu.VMEM((3 * blk, NA_W), BF16), pltpu.VMEM((3 * blk, NA_W), BF16)],
        compiler_params=pltpu.CompilerParams(
            dimension_semantics=("arbitrary", "arbitrary"), vmem_limit_bytes=_vmem_limit(est)),
        name="na",
    )(rest, rest, rest, rest, rest, rest, rest, bias_tbl)


def _post_kernel(x_ref, o_ref, og_ref, na_ref, gr_ref, gn_ref, wret_ref, wna_ref, wmix_ref, g_ref, b_ref, wr_ref,
                 br_ref, x1e_ref, *, alpha, sub_rows):
    for r0 in range(0, x_ref.shape[0], sub_rows):
        rows = slice(r0, r0 + sub_rows)
        gated = []
        for h in range(RET_HEADS):
            lanes = slice(h * RET_DV, (h + 1) * RET_DV)
            o = o_ref[rows, lanes].astype(F32)
            mu = jnp.mean(o, axis=-1, keepdims=True)
            d = o - mu
            var = jnp.mean(d * d, axis=-1, keepdims=True)
            og = og_ref[rows, lanes].astype(F32)
            gated.append((og * _sigmoid(og) * (d * lax.rsqrt(var + GN_EPS))).astype(BF16))
        ret = jnp.dot(jnp.concatenate(gated, axis=1), wret_ref[...], preferred_element_type=F32)
        na_p = jnp.dot(na_ref[rows, :], wna_ref[...], preferred_element_type=F32)
        merged = _sigmoid(gr_ref[rows, :].astype(F32)) * ret + _sigmoid(gn_ref[rows, :].astype(F32)) * na_p
        y = alpha * x_ref[rows, :] + jnp.dot(merged.astype(BF16), wmix_ref[...], preferred_element_type=F32)
        x1 = _layer_norm_rows(y, g_ref[...], b_ref[...])
        x1e_ref[rows, :D_MODEL] = x1
        logits = jnp.dot(x1.astype(BF16), wr_ref[...], preferred_element_type=F32) + br_ref[...]
        x1e_ref[rows, D_MODEL:] = _route_record(logits)


def _post(x2, o_ret, na, rest, wret, wna, wmix, ln_g, ln_b, wr, br, alpha):
    n = x2.shape[0]
    tm = min(512, n)
    est = (2 * tm * D_MODEL * (4 + 2 + 2 + 2) + 2 * tm * MOE_ROW_W * 4 + 2 * 2 * tm * RET_V_W * 2
           + tm * RET_V_W * 2 + (RET_V_W + 2 * D_MODEL + ROUTER_LANES) * D_MODEL * 2)
    row = lambda i: (i, 0)
    whole = pl.BlockSpec(memory_space=pltpu.VMEM)
    return pl.pallas_call(
        functools.partial(_post_kernel, alpha=alpha, sub_rows=min(POST_SUB_ROWS, tm)),
        out_shape=jax.ShapeDtypeStruct((n, MOE_ROW_W), F32),
        grid=(n // tm,),
        in_specs=[
            pl.BlockSpec((tm, D_MODEL), row),
            pl.BlockSpec((tm, RET_V_W), row),
            pl.BlockSpec((tm, RET_V_W), lambda i: (i, CB_GR * COL_TILE // RET_V_W)),
            pl.BlockSpec((tm, NA_W), row),
            pl.BlockSpec((tm, D_MODEL), lambda i: (i, CB_GATE_R)),
            pl.BlockSpec((tm, D_MODEL), lambda i: (i, CB_GATE_N)),
            whole, whole, whole, whole, whole, whole, whole,
        ],
        out_specs=pl.BlockSpec((tm, MOE_ROW_W), row),
        compiler_params=pltpu.CompilerParams(
            dimension_semantics=("arbitrary",), vmem_limit_bytes=_vmem_limit(est)),
        name="post",
    )(x2, o_ret, rest, na, rest, rest, wret, wna, wmix, ln_g, ln_b, wr, br)


def _route_record(logits):
    lt = logits.T
    gl = [lt[i:i + 1, :] for i in range(N_GROUPS)]
    gmax = functools.reduce(jnp.maximum, gl)
    denom = functools.reduce(lambda a, b: a + b, [jnp.exp(v - gmax) for v in gl])
    group_p = 1.0 / denom
    gidx = jnp.full(gmax.shape, N_GROUPS - 1, jnp.int32)
    for i in range(N_GROUPS - 2, -1, -1):
        gidx = jnp.where(gl[i] == gmax, i, gidx)
    el = []
    for e in range(EXPERTS_PER_GROUP):
        c = N_GROUPS + (N_GROUPS - 1) * EXPERTS_PER_GROUP + e
        v = lt[c:c + 1, :]
        for gi in range(N_GROUPS - 2, -1, -1):
            c = N_GROUPS + gi * EXPERTS_PER_GROUP + e
            v = jnp.where(gidx == gi, lt[c:c + 1, :], v)
        el.append(v)
    m1 = functools.reduce(jnp.maximum, el)
    i1 = jnp.full(m1.shape, EXPERTS_PER_GROUP - 1, jnp.int32)
    for e in range(EXPERTS_PER_GROUP - 2, -1, -1):
        i1 = jnp.where(el[e] == m1, e, i1)
    neg = jnp.float32(-jnp.inf)
    rest = [jnp.where(i1 == e, neg, el[e]) for e in range(EXPERTS_PER_GROUP)]
    m2 = functools.reduce(jnp.maximum, rest)
    i2 = jnp.full(m2.shape, EXPERTS_PER_GROUP - 1, jnp.int32)
    for e in range(EXPERTS_PER_GROUP - 2, -1, -1):
        i2 = jnp.where((rest[e] == m2) & (i1 != e), e, i2)
    e2 = jnp.exp(m2 - m1)
    w1 = group_p / (1.0 + e2)
    w2 = group_p * e2 / (1.0 + e2)
    field = lax.broadcasted_iota(jnp.int32, lt.shape, 0)
    rec = jnp.where(field == 0, gidx.astype(F32), 0.0)
    for e in range(EXPERTS_PER_GROUP):
        gate = jnp.where(i1 == e, w1, 0.0) + jnp.where(i2 == e, w2, 0.0)
        rec = jnp.where(field == 1 + e, gate, rec)
    return rec.T


def _moe_plan(gidx, tm):
    n = gidx.shape[0]
    n_tiles = (n // tm + N_GROUPS + 1) // 2 * 2
    groups = jnp.arange(N_GROUPS, dtype=jnp.int32)
    cnt = jnp.sum((gidx[:, None] == groups[None, :]).astype(jnp.int32), axis=0)
    tiles = (cnt + tm - 1) // tm
    tile_end = jnp.cumsum(tiles)
    tile_start = tile_end - tiles
    seg_start = jnp.cumsum(cnt) - cnt
    order = jnp.argsort(gidx, stable=True).astype(jnp.int32)
    t_ids = jnp.arange(n_tiles, dtype=jnp.int32)
    tgrp = jnp.minimum(jnp.sum((t_ids[:, None] >= tile_end[None, :]).astype(jnp.int32), axis=1), N_GROUPS - 1)
    first_rank = (t_ids - tile_start[tgrp]) * tm
    tcnt = jnp.clip(cnt[tgrp] - first_rank, 0, tm)
    p_local = jnp.arange(tm, dtype=jnp.int32)
    sorted_pos = seg_start[tgrp][:, None] + first_rank[:, None] + p_local[None, :]
    src = jnp.where(p_local[None, :] < tcnt[:, None], order[jnp.clip(sorted_pos, 0, n - 1)], 0)
    return src.reshape(n_tiles, 1, tm), tgrp.astype(jnp.int32), tcnt.astype(jnp.int32)


def _moe_kernel(tgrp_ref, tcnt_ref, src_ref, src_next_ref, x_hbm, w1a_ref, w3a_ref, w2a_ref, w1b_ref, w3b_ref, w2b_ref,
                g_ref, b_ref, out_hbm, xbuf, obuf, pad_sink, gsem, ssem, *, alpha, tm):
    del tgrp_ref
    s = pl.program_id(0)
    ns = pl.num_programs(0)

    def gather_starts(idx_ref, which, sl):
        def start_row(p):
            tok = idx_ref[which, 0, p]
            pltpu.make_async_copy(x_hbm.at[pl.ds(tok, 1), :], xbuf.at[sl, pl.ds(p, 1), :], gsem.at[sl]).start(
                priority=p % 2)
        return [functools.partial(start_row, p) for p in range(tm)]

    def start_gather(idx_ref, which, sl):
        for thunk in gather_starts(idx_ref, which, sl):
            thunk()

    def wait_gather(sl):
        pltpu.make_async_copy(x_hbm.at[pl.ds(0, tm), :], xbuf.at[sl], gsem.at[sl]).wait()

    def wait_scatter(sl):
        pltpu.make_async_copy(obuf.at[sl], out_hbm.at[pl.ds(0, tm), :], ssem.at[sl]).wait()

    def compute(sl, w1_ref, w3_ref, w2_ref):
        xe = xbuf[sl]
        x = xe[:, :D_MODEL]
        xb = x.astype(BF16)
        hs = []
        for e in range(EXPERTS_PER_GROUP):
            a = jnp.dot(xb, w1_ref[e], preferred_element_type=F32)
            b = jnp.dot(xb, w3_ref[e], preferred_element_type=F32)
            gate = xe[:, D_MODEL + 1 + e:D_MODEL + 2 + e]
            hs.append((a * _sigmoid(a) * b * gate).astype(BF16))
        w2 = w2_ref[...].reshape(EXPERTS_PER_GROUP * D_FF_EXPERT, D_MODEL)
        y = alpha * x + jnp.dot(jnp.concatenate(hs, axis=1), w2, preferred_element_type=F32)
        obuf[sl] = _layer_norm_rows(y, g_ref[...], b_ref[...])

    def sink_copy(p, sl):
        return pltpu.make_async_copy(obuf.at[sl, pl.ds(p, 1), :], pad_sink.at[sl, pl.ds(p, 1), :], ssem.at[sl])

    def scatter_row_copy(which, sl, p):
        tok = src_ref[which, 0, p]
        return pltpu.make_async_copy(obuf.at[sl, pl.ds(p, 1), :], out_hbm.at[pl.ds(tok, 1), :], ssem.at[sl])

    def scatter_starts(which, sl):
        def start_row(p):
            scatter_row_copy(which, sl, p).start(priority=p % 2)
        return [functools.partial(start_row, p) for p in range(tm)]

    def start_partial_scatter(which, sl):
        n_real = tcnt_ref[2 * s + which]

        def real_body(p, carry):
            scatter_row_copy(which, sl, p).start()
            return carry

        def sink_body(p, carry):
            sink_copy(p, sl).start()
            return carry

        lax.fori_loop(0, n_real, real_body, 0)
        lax.fori_loop(n_real, tm, sink_body, 0)

    def start_scatter(which, sl):
        n_real = tcnt_ref[2 * s + which]

        @pl.when(n_real == tm)
        def _():
            for thunk in scatter_starts(which, sl):
                thunk()

        @pl.when(n_real < tm)
        def _():
            start_partial_scatter(which, sl)

    @pl.when(s == 0)
    def _():
        start_gather(src_ref, 0, 0)

    wait_gather(0)
    start_gather(src_ref, 1, 1)

    @pl.when(s >= 1)
    def _():
        wait_scatter(0)

    compute(0, w1a_ref, w3a_ref, w2a_ref)
    start_scatter(0, 0)
    wait_gather(1)

    @pl.when(s + 1 < ns)
    def _():
        start_gather(src_next_ref, 0, 0)

    @pl.when(s >= 1)
    def _():
        wait_scatter(1)

    compute(1, w1b_ref, w3b_ref, w2b_ref)
    start_scatter(1, 1)

    @pl.when(s == ns - 1)
    def _():
        wait_scatter(0)
        wait_scatter(1)


def _moe(x1e, w1, w3, w2, ln_g, ln_b, alpha):
    n = x1e.shape[0]
    tm = min(MOE_TILE, n)
    gidx = x1e[:, D_MODEL].astype(jnp.int32)
    src, tgrp, tcnt = _moe_plan(gidx, tm)
    n_tiles = src.shape[0]
    assert n_tiles % 2 == 0
    est = (2 * tm * MOE_ROW_W * 4 + 4 * tm * D_MODEL * 4 + 2 * 2 * 3 * EXPERTS_PER_GROUP * D_MODEL * D_FF_EXPERT * 2
           + tm * 2 * EXPERTS_PER_GROUP * D_FF_EXPERT * 4)
    w_up = (EXPERTS_PER_GROUP, D_MODEL, D_FF_EXPERT)
    w_down = (EXPERTS_PER_GROUP, D_FF_EXPERT, D_MODEL)
    grid_spec = pltpu.PrefetchScalarGridSpec(
        num_scalar_prefetch=2,
        grid=(n_tiles // 2,),
        in_specs=[
            pl.BlockSpec((2, 1, tm), lambda s, tg, tc: (s, 0, 0), memory_space=pltpu.SMEM),
            pl.BlockSpec((1, 1, tm), lambda s, tg, tc: (jnp.minimum(2 * s + 2, n_tiles - 1), 0, 0),
                         memory_space=pltpu.SMEM),
            pl.BlockSpec(memory_space=pl.ANY),
            pl.BlockSpec(w_up, lambda s, tg, tc: (tg[2 * s], 0, 0)),
            pl.BlockSpec(w_up, lambda s, tg, tc: (tg[2 * s], 0, 0)),
            pl.BlockSpec(w_down, lambda s, tg, tc: (tg[2 * s], 0, 0)),
            pl.BlockSpec(w_up, lambda s, tg, tc: (tg[2 * s + 1], 0, 0)),
            pl.BlockSpec(w_up, lambda s, tg, tc: (tg[2 * s + 1], 0, 0)),
            pl.BlockSpec(w_down, lambda s, tg, tc: (tg[2 * s + 1], 0, 0)),
            pl.BlockSpec((1, D_MODEL), lambda s, tg, tc: (0, 0)),
            pl.BlockSpec((1, D_MODEL), lambda s, tg, tc: (0, 0)),
        ],
        out_specs=pl.BlockSpec(memory_space=pl.ANY),
        scratch_shapes=[
            pltpu.VMEM((2, tm, MOE_ROW_W), F32),
            pltpu.VMEM((2, tm, D_MODEL), F32),
            pltpu.VMEM((2, tm, D_MODEL), F32),
            pltpu.SemaphoreType.DMA((2,)),
            pltpu.SemaphoreType.DMA((2,)),
        ],
    )
    return pl.pallas_call(
        functools.partial(_moe_kernel, alpha=alpha, tm=tm),
        out_shape=jax.ShapeDtypeStruct((n, D_MODEL), F32),
        grid_spec=grid_spec,
        compiler_params=pltpu.CompilerParams(
            dimension_semantics=("arbitrary",), vmem_limit_bytes=_vmem_limit(est)),
        name="moe",
    )(tgrp, tcnt, src, src, x1e, w1, w3, w2, w1, w3, w2, ln_g, ln_b)


def _rotary_tables(seq_len):
    half = RET_DK // 2
    inv_freq = ROPE_BASE ** (-jnp.arange(half, dtype=F32) / half)
    ang = jnp.arange(seq_len, dtype=F32)[:, None] * inv_freq[None, :]
    return jnp.cos(ang), jnp.sin(ang)


def _prepare_layer(w_in, decay_fwd, decay_bwd, rpb, w_ret_out, w_na_out, w_mix_out, ln1_g, ln1_b, wg, bg, we, be, w1,
                   w3, w2, ln2_g, ln2_b):
    pad = ROUTER_LANES - N_GROUPS - N_EXPERTS
    wr = jnp.concatenate([wg, we.reshape(D_MODEL, N_EXPERTS), jnp.zeros((D_MODEL, pad), F32)], axis=1)
    br = jnp.concatenate([bg.astype(F32), be.reshape(N_EXPERTS).astype(F32), jnp.zeros((pad,), F32)])
    col_scale = jnp.ones((IN_COLS,), F32)
    col_scale = col_scale.at[RET_QK_W:2 * RET_QK_W].set(RET_DK ** -0.5)
    qn0 = QK_COLS + CB_QN * COL_TILE
    col_scale = col_scale.at[qn0:qn0 + NA_W].set(NA_HEAD_DIM ** -0.5)
    w_in = (w_in.astype(F32) * col_scale).astype(BF16)
    return dict(
        w_in=w_in,
        lg=jnp.stack([jax.nn.log_sigmoid(decay_fwd.astype(F32)), jax.nn.log_sigmoid(decay_bwd.astype(F32))]),
        bias_tbl=_na_bias_table(rpb),
        w_ret=w_ret_out.astype(BF16), w_na=w_na_out.astype(BF16), w_mix=w_mix_out.astype(BF16),
        ln1_g=ln1_g.astype(F32).reshape(1, D_MODEL), ln1_b=ln1_b.astype(F32).reshape(1, D_MODEL),
        wr=wr.astype(BF16), br=br.reshape(1, ROUTER_LANES),
        w1=w1.astype(BF16), w3=w3.astype(BF16), w2=w2.astype(BF16),
        ln2_g=ln2_g.astype(F32).reshape(1, D_MODEL), ln2_b=ln2_b.astype(F32).reshape(1, D_MODEL),
    )


def _encoder_layer(x2, p, batch, seq_len, cos, sin, alpha, chunk):
    qk, rest = _in_proj(x2, p["w_in"], cos, sin, seq_len)
    oa = _ret_fwd(qk, rest, p["lg"], batch, seq_len, chunk)
    o_ret = _ret_bwd(qk, rest, oa, p["lg"], batch, seq_len, chunk)
    na = _na(rest, p["bias_tbl"], batch, seq_len)
    x1e = _post(x2, o_ret, na, rest, p["w_ret"], p["w_na"], p["w_mix"], p["ln1_g"], p["ln1_b"], p["wr"], p["br"],
                alpha)
    return _moe(x1e, p["w1"], p["w3"], p["w2"], p["ln2_g"], p["ln2_b"], alpha)


def _trunk(x, layers, alpha, cos, sin, chunk=RET_CHUNK):
    batch, seq_len, _ = x.shape
    x2 = x.reshape(batch * seq_len, D_MODEL)
    for p in layers:
        x2 = _encoder_layer(x2, p, batch, seq_len, cos, sin, alpha, chunk)
    return x2.reshape(batch, seq_len, D_MODEL)


def kernel(x_prompt, x_sample, w_in, ret_decay_fwd, ret_decay_bwd, na_rel_bias, w_ret_out, w_na_out, w_mix_out, ln1_g,
           ln1_b, router_group_w, router_group_b, router_expert_w, router_expert_b, expert_w1, expert_w3, expert_w2,
           ln2_g, ln2_b):
    depth = w_in.shape[0]
    alpha = (2 * depth) ** 0.25
    per_layer = (w_in, ret_decay_fwd, ret_decay_bwd, na_rel_bias, w_ret_out, w_na_out, w_mix_out, ln1_g, ln1_b,
                 router_group_w, router_group_b, router_expert_w, router_expert_b, expert_w1, expert_w3, expert_w2,
                 ln2_g, ln2_b)
    layers = [_prepare_layer(*[a[l] for a in per_layer]) for l in range(depth)]
    cos, sin = _rotary_tables(max(x_prompt.shape[1], x_sample.shape[1]))
    return (_trunk(x_prompt, layers, alpha, cos, sin), _trunk(x_sample, layers, alpha, cos, sin))
```

```python
import functools

import numpy as np
import jax
import jax.numpy as jnp
from jax import lax
from jax.experimental import pallas as pl
from jax.experimental.pallas import tpu as pltpu

F32 = jnp.float32
BF16 = jnp.bfloat16

D_MODEL = 1024
GRID_W = 64
RET_HEADS = 4
RET_DK = 256
RET_DV = 512
RET_QK_W = RET_HEADS * RET_DK
RET_V_W = RET_HEADS * RET_DV
ROPE_BASE = 10000.0
RET_CHUNK = 256
NA_HEADS = 16
NA_HEAD_DIM = 64
NA_W = NA_HEADS * NA_HEAD_DIM
NA_WIN_ROWS = 8
NA_WIN_COLS = 16
IN_COLS = 2 * RET_QK_W + 2 * RET_V_W + 3 * NA_W + 2 * D_MODEL
N_GROUPS = 4
EXPERTS_PER_GROUP = 4
N_EXPERTS = N_GROUPS * EXPERTS_PER_GROUP
D_FF_EXPERT = 256
LN_EPS = 1e-5
GN_EPS = 1e-5
LOG2_E = 1.4426950408889634

COL_TILE = 1024
QK_COLS = 2 * RET_QK_W
REST_COLS = IN_COLS - QK_COLS
CB_QR, CB_KR = 0, 1
CB_VR, CB_GR, CB_QN, CB_KN, CB_VN, CB_GATE_R, CB_GATE_N = 0, 2, 4, 5, 6, 7, 8

ROUTER_LANES = 128
POST_SUB_ROWS = 256
MOE_ROW_W = D_MODEL + ROUTER_LANES
MOE_TILE = 512
IN_PROJ_ROWS = 2048
NA_ROWS_PER_STEP = 8
NA_HEADS_PER_GROUP = 4
NA_GROUP_W = NA_HEADS_PER_GROUP * NA_HEAD_DIM
NA_MASK_VALUE = -1e30
V7X_VMEM_BYTES = 64 * 1024 * 1024


def _vmem_limit(estimate_bytes):
    return int(min(V7X_VMEM_BYTES - 8 * 1024 * 1024, max(32 * 1024 * 1024, estimate_bytes * 3 // 2)))


def _sigmoid(x):
    return 1.0 / (1.0 + jnp.exp(-x))


def _layer_norm_rows(y, g, b):
    mu = jnp.mean(y, axis=-1, keepdims=True)
    d = y - mu
    var = jnp.mean(d * d, axis=-1, keepdims=True)
    return d * lax.rsqrt(var + LN_EPS) * g + b


def _in_proj_rotary_kernel(x_ref, w_ref, cos_ref, sin_ref, o_ref, xb_ref):
    @pl.when(pl.program_id(1) == 0)
    def _():
        xb_ref[...] = x_ref[...].astype(BF16)

    acc = jnp.dot(xb_ref[...], w_ref[...], preferred_element_type=F32)
    cos = cos_ref[...]
    sin = sin_ref[...]
    half = RET_DK // 2
    for h in range(RET_HEADS):
        x1 = acc[:, h * RET_DK:h * RET_DK + half]
        x2 = acc[:, h * RET_DK + half:(h + 1) * RET_DK]
        o_ref[:, h * RET_DK:h * RET_DK + half] = (x1 * cos - x2 * sin).astype(BF16)
        o_ref[:, h * RET_DK + half:(h + 1) * RET_DK] = (x1 * sin + x2 * cos).astype(BF16)


def _in_proj_plain_kernel(x_ref, w_ref, o_ref, xb_ref):
    @pl.when(pl.program_id(1) == 0)
    def _():
        xb_ref[...] = x_ref[...].astype(BF16)

    o_ref[...] = jnp.dot(xb_ref[...], w_ref[...], preferred_element_type=F32).astype(BF16)


def _in_proj(x2, w_in, cos, sin, seq_len):
    n = x2.shape[0]
    tm = min(IN_PROJ_ROWS, seq_len)
    tiles_per_seq = seq_len // tm
    qk_blocks = QK_COLS // COL_TILE
    est = (2 * tm * D_MODEL * 4 + tm * D_MODEL * 2 + 2 * D_MODEL * COL_TILE * 2 + 2 * tm * COL_TILE * 2
           + tm * COL_TILE * 4)
    params = pltpu.CompilerParams(dimension_semantics=("arbitrary", "arbitrary"), vmem_limit_bytes=_vmem_limit(est))
    x_spec = pl.BlockSpec((tm, D_MODEL), lambda i, j: (i, 0))
    w_spec = pl.BlockSpec((D_MODEL, COL_TILE), lambda i, j: (0, j))
    o_spec = pl.BlockSpec((tm, COL_TILE), lambda i, j: (i, j))
    rot_spec = pl.BlockSpec((tm, RET_DK // 2), lambda i, j: (i % tiles_per_seq, 0))
    qk = pl.pallas_call(
        _in_proj_rotary_kernel,
        out_shape=jax.ShapeDtypeStruct((n, QK_COLS), BF16),
        grid=(n // tm, QK_COLS // COL_TILE),
        in_specs=[x_spec, w_spec, rot_spec, rot_spec],
        out_specs=o_spec,
        scratch_shapes=[pltpu.VMEM((tm, D_MODEL), BF16)],
        compiler_params=params,
        name="in_proj_qk",
    )(x2, w_in, cos, sin)
    rest = pl.pallas_call(
        _in_proj_plain_kernel,
        out_shape=jax.ShapeDtypeStruct((n, REST_COLS), BF16),
        grid=(n // tm, REST_COLS // COL_TILE),
        in_specs=[x_spec, pl.BlockSpec((D_MODEL, COL_TILE), lambda i, j: (0, j + qk_blocks))],
        out_specs=o_spec,
        scratch_shapes=[pltpu.VMEM((tm, D_MODEL), BF16)],
        compiler_params=params,
        name="in_proj_rest",
    )(x2, w_in)
    return qk, rest


def _ret_tables(lg_ref, chunk, dmat_ref, qd_ref, kd_ref, cd_ref, direction):
    ii = lax.broadcasted_iota(jnp.int32, (chunk, chunk), 0)
    jj = lax.broadcasted_iota(jnp.int32, (chunk, chunk), 1)
    diff = (ii - jj).astype(F32)
    pos_q = lax.broadcasted_iota(jnp.int32, (chunk, RET_DV), 0).astype(F32)
    pos_k = lax.broadcasted_iota(jnp.int32, (chunk, RET_DK), 0).astype(F32)
    for h in range(RET_HEADS):
        lf = lg_ref[0, h]
        lb = lg_ref[1, h]
        if dmat_ref is not None:
            dmat_ref[h] = jnp.where(diff >= 0, jnp.exp(jnp.maximum(diff, 0.0) * lf),
                                    jnp.exp(jnp.maximum(-diff, 0.0) * lb))
        if direction == 0:
            qd_ref[h] = jnp.exp((pos_q + 1.0) * lf)
            kd_ref[h] = jnp.exp((chunk - 1.0 - pos_k) * lf)
            cd_ref[h] = jnp.exp(jnp.zeros((8, RET_DV), F32) + chunk * lf)
        else:
            qd_ref[h] = jnp.exp((chunk - pos_q) * lb)
            kd_ref[h] = jnp.exp(pos_k * lb)
            cd_ref[h] = jnp.exp(jnp.zeros((8, RET_DV), F32) + chunk * lb)


def _ret_cross_and_update(q, k, v, h, state_ref, qd_ref, kd_ref, cd_ref):
    st = state_ref[h]
    cross = jnp.dot(q, st.astype(BF16), preferred_element_type=F32) * qd_ref[h]
    kd = (k.astype(F32) * kd_ref[h]).astype(BF16)
    upd = lax.dot_general(kd, v, (((0,), (0,)), ((), ())), preferred_element_type=F32)
    state_ref[h] = st * cd_ref[h, 0:1, :] + upd
    return cross


def _ret_fwd_kernel(lg_ref, q_ref, k_ref, v_ref, o_ref, state_ref, dmat_ref, qd_ref, kd_ref, cd_ref, *, chunk,
                    n_chunks):
    first = (pl.program_id(0) == 0) & (pl.program_id(1) == 0)

    @pl.when(first)
    def _():
        _ret_tables(lg_ref, chunk, dmat_ref, qd_ref, kd_ref, cd_ref, 0)

    @pl.when(pl.program_id(1) == 0)
    def _():
        state_ref[...] = jnp.zeros(state_ref.shape, F32)

    def body(c, carry):
        rows = pl.ds(pl.multiple_of(c * chunk, chunk), chunk)
        for h in range(RET_HEADS):
            q = q_ref[rows, h * RET_DK:(h + 1) * RET_DK]
            k = k_ref[rows, h * RET_DK:(h + 1) * RET_DK]
            v = v_ref[rows, h * RET_DV:(h + 1) * RET_DV]
            s = lax.dot_general(q, k, (((1,), (1,)), ((), ())), preferred_element_type=F32)
            p = (s * dmat_ref[h]).astype(BF16)
            o = jnp.dot(p, v, preferred_element_type=F32)
            o = o + _ret_cross_and_update(q, k, v, h, state_ref, qd_ref, kd_ref, cd_ref)
            o_ref[rows, h * RET_DV:(h + 1) * RET_DV] = o.astype(o_ref.dtype)
        return carry

    lax.fori_loop(0, n_chunks, body, 0)


def _ret_bwd_kernel(lg_ref, q_ref, k_ref, v_ref, oa_ref, o_ref, state_ref, qd_ref, kd_ref, cd_ref, *, chunk, n_chunks):
    first = (pl.program_id(0) == 0) & (pl.program_id(1) == 0)

    @pl.when(first)
    def _():
        _ret_tables(lg_ref, chunk, None, qd_ref, kd_ref, cd_ref, 1)

    @pl.when(pl.program_id(1) == 0)
    def _():
        state_ref[...] = jnp.zeros(state_ref.shape, F32)

    def body(cc, carry):
        c = n_chunks - 1 - cc
        rows = pl.ds(pl.multiple_of(c * chunk, chunk), chunk)
        for h in range(RET_HEADS):
            q = q_ref[rows, h * RET_DK:(h + 1) * RET_DK]
            k = k_ref[rows, h * RET_DK:(h + 1) * RET_DK]
            v = v_ref[rows, h * RET_DV:(h + 1) * RET_DV]
            o = oa_ref[rows, h * RET_DV:(h + 1) * RET_DV].astype(F32)
            o = o + _ret_cross_and_update(q, k, v, h, state_ref, qd_ref, kd_ref, cd_ref)
            o_ref[rows, h * RET_DV:(h + 1) * RET_DV] = o.astype(o_ref.dtype)
        return carry

    lax.fori_loop(0, n_chunks, body, 0)


def _ret_step_tokens(seq_len, chunk):
    return min(512, seq_len) // chunk * chunk


def _ret_fwd(qk, rest, lg, batch, seq_len, chunk):
    ts = _ret_step_tokens(seq_len, chunk)
    nt = seq_len // ts
    n = batch * seq_len
    est = (2 * (2 * ts * RET_QK_W * 2 + ts * RET_V_W * 2 + ts * RET_V_W * 2)
           + RET_HEADS * (RET_DK * RET_DV * 4 + chunk * chunk * 4 + chunk * RET_DV * 4 + chunk * RET_DK * 4))
    kern = functools.partial(_ret_fwd_kernel, chunk=chunk, n_chunks=ts // chunk)
    return pl.pallas_call(
        kern,
        out_shape=jax.ShapeDtypeStruct((n, RET_V_W), BF16),
        grid=(batch, nt),
        in_specs=[
            pl.BlockSpec(memory_space=pltpu.SMEM),
            pl.BlockSpec((ts, RET_QK_W), lambda b, t: (b * nt + t, CB_QR)),
            pl.BlockSpec((ts, RET_QK_W), lambda b, t: (b * nt + t, CB_KR)),
            pl.BlockSpec((ts, RET_V_W), lambda b, t: (b * nt + t, CB_VR * COL_TILE // RET_V_W)),
        ],
        out_specs=pl.BlockSpec((ts, RET_V_W), lambda b, t: (b * nt + t, 0)),
        scratch_shapes=[
            pltpu.VMEM((RET_HEADS, RET_DK, RET_DV), F32),
            pltpu.VMEM((RET_HEADS, chunk, chunk), F32),
            pltpu.VMEM((RET_HEADS, chunk, RET_DV), F32),
            pltpu.VMEM((RET_HEADS, chunk, RET_DK), F32),
            pltpu.VMEM((RET_HEADS, 8, RET_DV), F32),
        ],
        compiler_params=pltpu.CompilerParams(
            dimension_semantics=("arbitrary", "arbitrary"), vmem_limit_bytes=_vmem_limit(est)),
        name="ret_fwd",
    )(lg, qk, qk, rest)


def _ret_bwd(qk, rest, oa, lg, batch, seq_len, chunk):
    ts = _ret_step_tokens(seq_len, chunk)
    nt = seq_len // ts
    n = batch * seq_len
    est = (2 * (2 * ts * RET_QK_W * 2 + 3 * ts * RET_V_W * 2)
           + RET_HEADS * (RET_DK * RET_DV * 4 + chunk * RET_DV * 4 + chunk * RET_DK * 4))
    kern = functools.partial(_ret_bwd_kernel, chunk=chunk, n_chunks=ts // chunk)

    def rev(b, t):
        return b * nt + (nt - 1 - t)

    return pl.pallas_call(
        kern,
        out_shape=jax.ShapeDtypeStruct((n, RET_V_W), BF16),
        grid=(batch, nt),
        in_specs=[
            pl.BlockSpec(memory_space=pltpu.SMEM),
            pl.BlockSpec((ts, RET_QK_W), lambda b, t: (rev(b, t), CB_QR)),
            pl.BlockSpec((ts, RET_QK_W), lambda b, t: (rev(b, t), CB_KR)),
            pl.BlockSpec((ts, RET_V_W), lambda b, t: (rev(b, t), CB_VR * COL_TILE // RET_V_W)),
            pl.BlockSpec((ts, RET_V_W), lambda b, t: (rev(b, t), 0)),
        ],
        out_specs=pl.BlockSpec((ts, RET_V_W), lambda b, t: (rev(b, t), 0)),
        scratch_shapes=[
            pltpu.VMEM((RET_HEADS, RET_DK, RET_DV), F32),
            pltpu.VMEM((RET_HEADS, chunk, RET_DV), F32),
            pltpu.VMEM((RET_HEADS, chunk, RET_DK), F32),
            pltpu.VMEM((RET_HEADS, 8, RET_DV), F32),
        ],
        compiler_params=pltpu.CompilerParams(
            dimension_semantics=("arbitrary", "arbitrary"), vmem_limit_bytes=_vmem_limit(est)),
        name="ret_bwd",
    )(lg, qk, qk, rest, oa)


def _na_bias_table(rpb):
    cols = np.arange(GRID_W)
    col_start = np.clip(cols - NA_WIN_COLS // 2, 0, GRID_W - NA_WIN_COLS)
    valid = (cols[None, :] >= col_start[:, None]) & (cols[None, :] < col_start[:, None] + NA_WIN_COLS)
    pad = GRID_W
    padded = jnp.pad(rpb.astype(F32) * LOG2_E, ((0, 0), (0, 0), (pad, pad)))
    tb = jnp.stack([padded[:, :, pad + NA_WIN_COLS - 1 - c:pad + NA_WIN_COLS - 1 - c + GRID_W] for c in range(GRID_W)],
                   axis=1)
    tb = jnp.where(valid[None, :, None, :], tb, NA_MASK_VALUE)
    return jnp.stack([tb[:, :, s:s + NA_WIN_ROWS].reshape(NA_HEADS * GRID_W, NA_WIN_ROWS * GRID_W)
                      for s in range(NA_WIN_ROWS)], 0)


def _na_kernel(q_ref, kp_ref, kc_ref, kn_ref, vp_ref, vc_ref, vn_ref, bias_ref, o_ref, kbuf, vbuf, *, grid_rows):
    t = pl.program_id(1)
    blk = NA_ROWS_PER_STEP * GRID_W
    win = NA_WIN_ROWS * GRID_W
    lo = (NA_ROWS_PER_STEP - NA_WIN_ROWS // 2) * GRID_W
    hi = (NA_WIN_ROWS // 2 - 1) * GRID_W
    for buf, prev_ref, cur_ref, next_ref in ((kbuf, kp_ref, kc_ref, kn_ref), (vbuf, vp_ref, vc_ref, vn_ref)):
        buf[lo:blk] = prev_ref[lo:blk]
        buf[blk:2 * blk] = cur_ref[...]
        buf[2 * blk:2 * blk + hi] = next_ref[0:hi]

    rr = lax.broadcasted_iota(jnp.int32, (NA_GROUP_W, NA_GROUP_W), 0) // GRID_W
    ll = lax.broadcasted_iota(jnp.int32, (NA_GROUP_W, NA_GROUP_W), 1) // NA_HEAD_DIM
    head_mask = rr == ll

    def body(i, carry):
        r = t * NA_ROWS_PER_STEP + i
        r0 = jnp.clip(r - NA_WIN_ROWS // 2, 0, grid_rows - NA_WIN_ROWS)
        shift = r0 - r + NA_WIN_ROWS - 1
        off = pl.multiple_of((r0 - (t - 1) * NA_ROWS_PER_STEP) * GRID_W, GRID_W)
        qrows = pl.ds(pl.multiple_of(i * GRID_W, GRID_W), GRID_W)
        groups = [slice(g * NA_GROUP_W, (g + 1) * NA_GROUP_W) for g in range(NA_HEADS // NA_HEADS_PER_GROUP)]
        scores = []
        for lanes in groups:
            q4 = q_ref[qrows, lanes]
            qm = jnp.where(head_mask, jnp.concatenate([q4] * NA_HEADS_PER_GROUP, axis=0), jnp.zeros((), BF16))
            k4 = kbuf[pl.ds(off, win), lanes]
            sc = lax.dot_general(qm, k4, (((1,), (1,)), ((), ())), preferred_element_type=F32)
            scores.append(sc + bias_ref[shift, lanes, :])
        probs = []
        for sc in scores:
            p = jnp.exp2(sc - jnp.max(sc, axis=-1, keepdims=True))
            probs.append((p.astype(BF16), jnp.sum(p, axis=-1, keepdims=True)))
        for lanes, (pb, l) in zip(groups, probs):
            v4 = vbuf[pl.ds(off, win), lanes]
            oall = jnp.dot(pb, v4, preferred_element_type=F32) * (1.0 / l)
            oall = jnp.where(head_mask, oall, 0.0)
            o4 = oall[0:GRID_W]
            for hh in range(1, NA_HEADS_PER_GROUP):
                o4 = o4 + oall[hh * GRID_W:(hh + 1) * GRID_W]
            o_ref[qrows, lanes] = o4.astype(o_ref.dtype)
        return carry

    lax.fori_loop(0, NA_ROWS_PER_STEP, body, 0, unroll=4)


def _na(rest, bias_tbl, batch, seq_len):
    grid_rows = seq_len // GRID_W
    assert grid_rows % NA_ROWS_PER_STEP == 0 and grid_rows >= NA_WIN_ROWS
    nb = grid_rows // NA_ROWS_PER_STEP
    blk = NA_ROWS_PER_STEP * GRID_W
    n = batch * seq_len
    est = (2 * 7 * blk * NA_W * 2 + 2 * blk * NA_W * 2 + 2 * 3 * blk * NA_W * 2
           + NA_WIN_ROWS * NA_W * NA_WIN_ROWS * GRID_W * 4)

    def cur(b, t):
        return b * nb + t

    def prev(b, t):
        return b * nb + jnp.maximum(t - 1, 0)

    def nxt(b, t):
        return b * nb + jnp.minimum(t + 1, nb - 1)

    kern = functools.partial(_na_kernel, grid_rows=grid_rows)
    return pl.pallas_call(
        kern,
        out_shape=jax.ShapeDtypeStruct((n, NA_W), BF16),
        grid=(batch, nb),
        in_specs=[
            pl.BlockSpec((blk, NA_W), lambda b, t: (cur(b, t), CB_QN)),
            pl.BlockSpec((blk, NA_W), lambda b, t: (prev(b, t), CB_KN)),
            pl.BlockSpec((blk, NA_W), lambda b, t: (cur(b, t), CB_KN)),
            pl.BlockSpec((blk, NA_W), lambda b, t: (nxt(b, t), CB_KN)),
            pl.BlockSpec((blk, NA_W), lambda b, t: (prev(b, t), CB_VN)),
            pl.BlockSpec((blk, NA_W), lambda b, t: (cur(b, t), CB_VN)),
            pl.BlockSpec((blk, NA_W), lambda b, t: (nxt(b, t), CB_VN)),
            pl.BlockSpec(memory_space=pltpu.VMEM),
        ],
        out_specs=pl.BlockSpec((blk, NA_W), lambda b, t: (cur(b, t), 0)),
        scratch_shapes=[pltpu.VMEM((3 * blk, NA_W), BF16), pltpu.VMEM((3 * blk, NA_W), BF16)],
        compiler_params=pltpu.CompilerParams(
            dimension_semantics=("arbitrary", "arbitrary"), vmem_limit_bytes=_vmem_limit(est)),
        name="na",
    )(rest, rest, rest, rest, rest, rest, rest, bias_tbl)


def _post_kernel(x_ref, o_ref, og_ref, na_ref, gr_ref, gn_ref, wret_ref, wna_ref, wmix_ref, g_ref, b_ref, wr_ref,
                 br_ref, x1e_ref, *, alpha, sub_rows):
    for r0 in range(0, x_ref.shape[0], sub_rows):
        rows = slice(r0, r0 + sub_rows)
        gated = []
        for h in range(RET_HEADS):
            lanes = slice(h * RET_DV, (h + 1) * RET_DV)
            o = o_ref[rows, lanes].astype(F32)
            mu = jnp.mean(o, axis=-1, keepdims=True)
            d = o - mu
            var = jnp.mean(d * d, axis=-1, keepdims=True)
            og = og_ref[rows, lanes].astype(F32)
            gated.append((og * _sigmoid(og) * (d * lax.rsqrt(var + GN_EPS))).astype(BF16))
        ret = jnp.dot(jnp.concatenate(gated, axis=1), wret_ref[...], preferred_element_type=F32)
        na_p = jnp.dot(na_ref[rows, :], wna_ref[...], preferred_element_type=F32)
        merged = _sigmoid(gr_ref[rows, :].astype(F32)) * ret + _sigmoid(gn_ref[rows, :].astype(F32)) * na_p
        y = alpha * x_ref[rows, :] + jnp.dot(merged.astype(BF16), wmix_ref[...], preferred_element_type=F32)
        x1 = _layer_norm_rows(y, g_ref[...], b_ref[...])
        x1e_ref[rows, :D_MODEL] = x1
        logits = jnp.dot(x1.astype(BF16), wr_ref[...], preferred_element_type=F32) + br_ref[...]
        x1e_ref[rows, D_MODEL:] = _route_record(logits)


def _post(x2, o_ret, na, rest, wret, wna, wmix, ln_g, ln_b, wr, br, alpha):
    n = x2.shape[0]
    tm = min(512, n)
    est = (2 * tm * D_MODEL * (4 + 2 + 2 + 2) + 2 * tm * MOE_ROW_W * 4 + 2 * 2 * tm * RET_V_W * 2
           + tm * RET_V_W * 2 + (RET_V_W + 2 * D_MODEL + ROUTER_LANES) * D_MODEL * 2)
    row = lambda i: (i, 0)
    whole = pl.BlockSpec(memory_space=pltpu.VMEM)
    return pl.pallas_call(
        functools.partial(_post_kernel, alpha=alpha, sub_rows=min(POST_SUB_ROWS, tm)),
        out_shape=jax.ShapeDtypeStruct((n, MOE_ROW_W), F32),
        grid=(n // tm,),
        in_specs=[
            pl.BlockSpec((tm, D_MODEL), row),
            pl.BlockSpec((tm, RET_V_W), row),
            pl.BlockSpec((tm, RET_V_W), lambda i: (i, CB_GR * COL_TILE // RET_V_W)),
            pl.BlockSpec((tm, NA_W), row),
            pl.BlockSpec((tm, D_MODEL), lambda i: (i, CB_GATE_R)),
            pl.BlockSpec((tm, D_MODEL), lambda i: (i, CB_GATE_N)),
            whole, whole, whole, whole, whole, whole, whole,
        ],
        out_specs=pl.BlockSpec((tm, MOE_ROW_W), row),
        compiler_params=pltpu.CompilerParams(
            dimension_semantics=("arbitrary",), vmem_limit_bytes=_vmem_limit(est)),
        name="post",
    )(x2, o_ret, rest, na, rest, rest, wret, wna, wmix, ln_g, ln_b, wr, br)


def _route_record(logits):
    lt = logits.T
    gl = [lt[i:i + 1, :] for i in range(N_GROUPS)]
    gmax = functools.reduce(jnp.maximum, gl)
    denom = functools.reduce(lambda a, b: a + b, [jnp.exp(v - gmax) for v in gl])
    group_p = 1.0 / denom
    gidx = jnp.full(gmax.shape, N_GROUPS - 1, jnp.int32)
    for i in range(N_GROUPS - 2, -1, -1):
        gidx = jnp.where(gl[i] == gmax, i, gidx)
    el = []
    for e in range(EXPERTS_PER_GROUP):
        c = N_GROUPS + (N_GROUPS - 1) * EXPERTS_PER_GROUP + e
        v = lt[c:c + 1, :]
        for gi in range(N_GROUPS - 2, -1, -1):
            c = N_GROUPS + gi * EXPERTS_PER_GROUP + e
            v = jnp.where(gidx == gi, lt[c:c + 1, :], v)
        el.append(v)
    m1 = functools.reduce(jnp.maximum, el)
    i1 = jnp.full(m1.shape, EXPERTS_PER_GROUP - 1, jnp.int32)
    for e in range(EXPERTS_PER_GROUP - 2, -1, -1):
        i1 = jnp.where(el[e] == m1, e, i1)
    neg = jnp.float32(-jnp.inf)
    rest = [jnp.where(i1 == e, neg, el[e]) for e in range(EXPERTS_PER_GROUP)]
    m2 = functools.reduce(jnp.maximum, rest)
    i2 = jnp.full(m2.shape, EXPERTS_PER_GROUP - 1, jnp.int32)
    for e in range(EXPERTS_PER_GROUP - 2, -1, -1):
        i2 = jnp.where((rest[e] == m2) & (i1 != e), e, i2)
    e2 = jnp.exp(m2 - m1)
    w1 = group_p / (1.0 + e2)
    w2 = group_p * e2 / (1.0 + e2)
    field = lax.broadcasted_iota(jnp.int32, lt.shape, 0)
    rec = jnp.where(field == 0, gidx.astype(F32), 0.0)
    for e in range(EXPERTS_PER_GROUP):
        gate = jnp.where(i1 == e, w1, 0.0) + jnp.where(i2 == e, w2, 0.0)
        rec = jnp.where(field == 1 + e, gate, rec)
    return rec.T


def _moe_plan(gidx, tm):
    n = gidx.shape[0]
    n_tiles = (n // tm + N_GROUPS + 1) // 2 * 2
    groups = jnp.arange(N_GROUPS, dtype=jnp.int32)
    cnt = jnp.sum((gidx[:, None] == groups[None, :]).astype(jnp.int32), axis=0)
    tiles = (cnt + tm - 1) // tm
    tile_end = jnp.cumsum(tiles)
    tile_start = tile_end - tiles
    seg_start = jnp.cumsum(cnt) - cnt
    order = jnp.argsort(gidx, stable=True).astype(jnp.int32)
    t_ids = jnp.arange(n_tiles, dtype=jnp.int32)
    tgrp = jnp.minimum(jnp.sum((t_ids[:, None] >= tile_end[None, :]).astype(jnp.int32), axis=1), N_GROUPS - 1)
    first_rank = (t_ids - tile_start[tgrp]) * tm
    tcnt = jnp.clip(cnt[tgrp] - first_rank, 0, tm)
    p_local = jnp.arange(tm, dtype=jnp.int32)
    sorted_pos = seg_start[tgrp][:, None] + first_rank[:, None] + p_local[None, :]
    src = jnp.where(p_local[None, :] < tcnt[:, None], order[jnp.clip(sorted_pos, 0, n - 1)], 0)
    return src.reshape(n_tiles, 1, tm), tgrp.astype(jnp.int32), tcnt.astype(jnp.int32)


def _moe_kernel(tgrp_ref, tcnt_ref, src_ref, src_next_ref, x_hbm, w1a_ref, w3a_ref, w2a_ref, w1b_ref, w3b_ref, w2b_ref,
                g_ref, b_ref, out_hbm, xbuf, obuf, pad_sink, gsem, ssem, *, alpha, tm):
    del tgrp_ref
    s = pl.program_id(0)
    ns = pl.num_programs(0)

    def gather_starts(idx_ref, which, sl):
        def start_row(p):
            tok = idx_ref[which, 0, p]
            pltpu.make_async_copy(x_hbm.at[pl.ds(tok, 1), :], xbuf.at[sl, pl.ds(p, 1), :], gsem.at[sl]).start(
                priority=p % 2)
        return [functools.partial(start_row, p) for p in range(tm)]

    def start_gather(idx_ref, which, sl):
        for thunk in gather_starts(idx_ref, which, sl):
            thunk()

    def wait_gather(sl):
        pltpu.make_async_copy(x_hbm.at[pl.ds(0, tm), :], xbuf.at[sl], gsem.at[sl]).wait()

    def wait_scatter(sl):
        pltpu.make_async_copy(obuf.at[sl], out_hbm.at[pl.ds(0, tm), :], ssem.at[sl]).wait()

    def compute(sl, w1_ref, w3_ref, w2_ref):
        xe = xbuf[sl]
        x = xe[:, :D_MODEL]
        xb = x.astype(BF16)
        hs = []
        for e in range(EXPERTS_PER_GROUP):
            a = jnp.dot(xb, w1_ref[e], preferred_element_type=F32)
            b = jnp.dot(xb, w3_ref[e], preferred_element_type=F32)
            gate = xe[:, D_MODEL + 1 + e:D_MODEL + 2 + e]
            hs.append((a * _sigmoid(a) * b * gate).astype(BF16))
        w2 = w2_ref[...].reshape(EXPERTS_PER_GROUP * D_FF_EXPERT, D_MODEL)
        y = alpha * x + jnp.dot(jnp.concatenate(hs, axis=1), w2, preferred_element_type=F32)
        obuf[sl] = _layer_norm_rows(y, g_ref[...], b_ref[...])

    def sink_copy(p, sl):
        return pltpu.make_async_copy(obuf.at[sl, pl.ds(p, 1), :], pad_sink.at[sl, pl.ds(p, 1), :], ssem.at[sl])

    def scatter_row_copy(which, sl, p):
        tok = src_ref[which, 0, p]
        return pltpu.make_async_copy(obuf.at[sl, pl.ds(p, 1), :], out_hbm.at[pl.ds(tok, 1), :], ssem.at[sl])

    def scatter_starts(which, sl):
        def start_row(p):
            scatter_row_copy(which, sl, p).start(priority=p % 2)
        return [functools.partial(start_row, p) for p in range(tm)]

    def start_partial_scatter(which, sl):
        n_real = tcnt_ref[2 * s + which]

        def real_body(p, carry):
            scatter_row_copy(which, sl, p).start()
            return carry

        def sink_body(p, carry):
            sink_copy(p, sl).start()
            return carry

        lax.fori_loop(0, n_real, real_body, 0)
        lax.fori_loop(n_real, tm, sink_body, 0)

    def start_scatter(which, sl):
        n_real = tcnt_ref[2 * s + which]

        @pl.when(n_real == tm)
        def _():
            for thunk in scatter_starts(which, sl):
                thunk()

        @pl.when(n_real < tm)
        def _():
            start_partial_scatter(which, sl)

    @pl.when(s == 0)
    def _():
        start_gather(src_ref, 0, 0)

    wait_gather(0)
    start_gather(src_ref, 1, 1)

    @pl.when(s >= 1)
    def _():
        wait_scatter(0)

    compute(0, w1a_ref, w3a_ref, w2a_ref)
    start_scatter(0, 0)
    wait_gather(1)

    @pl.when(s + 1 < ns)
    def _():
        start_gather(src_next_ref, 0, 0)

    @pl.when(s >= 1)
    def _():
        wait_scatter(1)

    compute(1, w1b_ref, w3b_ref, w2b_ref)
    start_scatter(1, 1)

    @pl.when(s == ns - 1)
    def _():
        wait_scatter(0)
        wait_scatter(1)


def _moe(x1e, w1, w3, w2, ln_g, ln_b, alpha):
    n = x1e.shape[0]
    tm = min(MOE_TILE, n)
    gidx = x1e[:, D_MODEL].astype(jnp.int32)
    src, tgrp, tcnt = _moe_plan(gidx, tm)
    n_tiles = src.shape[0]
    assert n_tiles % 2 == 0
    est = (2 * tm * MOE_ROW_W * 4 + 4 * tm * D_MODEL * 4 + 2 * 2 * 3 * EXPERTS_PER_GROUP * D_MODEL * D_FF_EXPERT * 2
           + tm * 2 * EXPERTS_PER_GROUP * D_FF_EXPERT * 4)
    w_up = (EXPERTS_PER_GROUP, D_MODEL, D_FF_EXPERT)
    w_down = (EXPERTS_PER_GROUP, D_FF_EXPERT, D_MODEL)
    grid_spec = pltpu.PrefetchScalarGridSpec(
        num_scalar_prefetch=2,
        grid=(n_tiles // 2,),
        in_specs=[
            pl.BlockSpec((2, 1, tm), lambda s, tg, tc: (s, 0, 0), memory_space=pltpu.SMEM),
            pl.BlockSpec((1, 1, tm), lambda s, tg, tc: (jnp.minimum(2 * s + 2, n_tiles - 1), 0, 0),
                         memory_space=pltpu.SMEM),
            pl.BlockSpec(memory_space=pl.ANY),
            pl.BlockSpec(w_up, lambda s, tg, tc: (tg[2 * s], 0, 0)),
            pl.BlockSpec(w_up, lambda s, tg, tc: (tg[2 * s], 0, 0)),
            pl.BlockSpec(w_down, lambda s, tg, tc: (tg[2 * s], 0, 0)),
            pl.BlockSpec(w_up, lambda s, tg, tc: (tg[2 * s + 1], 0, 0)),
            pl.BlockSpec(w_up, lambda s, tg, tc: (tg[2 * s + 1], 0, 0)),
            pl.BlockSpec(w_down, lambda s, tg, tc: (tg[2 * s + 1], 0, 0)),
            pl.BlockSpec((1, D_MODEL), lambda s, tg, tc: (0, 0)),
            pl.BlockSpec((1, D_MODEL), lambda s, tg, tc: (0, 0)),
        ],
        out_specs=pl.BlockSpec(memory_space=pl.ANY),
        scratch_shapes=[
            pltpu.VMEM((2, tm, MOE_ROW_W), F32),
            pltpu.VMEM((2, tm, D_MODEL), F32),
            pltpu.VMEM((2, tm, D_MODEL), F32),
            pltpu.SemaphoreType.DMA((2,)),
            pltpu.SemaphoreType.DMA((2,)),
        ],
    )
    return pl.pallas_call(
        functools.partial(_moe_kernel, alpha=alpha, tm=tm),
        out_shape=jax.ShapeDtypeStruct((n, D_MODEL), F32),
        grid_spec=grid_spec,
        compiler_params=pltpu.CompilerParams(
            dimension_semantics=("arbitrary",), vmem_limit_bytes=_vmem_limit(est)),
        name="moe",
    )(tgrp, tcnt, src, src, x1e, w1, w3, w2, w1, w3, w2, ln_g, ln_b)


def _rotary_tables(seq_len):
    half = RET_DK // 2
    inv_freq = ROPE_BASE ** (-jnp.arange(half, dtype=F32) / half)
    ang = jnp.arange(seq_len, dtype=F32)[:, None] * inv_freq[None, :]
    return jnp.cos(ang), jnp.sin(ang)


def _prepare_layer(w_in, decay_fwd, decay_bwd, rpb, w_ret_out, w_na_out, w_mix_out, ln1_g, ln1_b, wg, bg, we, be, w1,
                   w3, w2, ln2_g, ln2_b):
    pad = ROUTER_LANES - N_GROUPS - N_EXPERTS
    wr = jnp.concatenate([wg, we.reshape(D_MODEL, N_EXPERTS), jnp.zeros((D_MODEL, pad), F32)], axis=1)
    br = jnp.concatenate([bg.astype(F32), be.reshape(N_EXPERTS).astype(F32), jnp.zeros((pad,), F32)])
    col_scale = jnp.ones((IN_COLS,), F32)
    col_scale = col_scale.at[RET_QK_W:2 * RET_QK_W].set(RET_DK ** -0.5)
    qn0 = QK_COLS + CB_QN * COL_TILE
    col_scale = col_scale.at[qn0:qn0 + NA_W].set(NA_HEAD_DIM ** -0.5 * LOG2_E)
    w_in = (w_in.astype(F32) * col_scale).astype(BF16)
    return dict(
        w_in=w_in,
        lg=jnp.stack([jax.nn.log_sigmoid(decay_fwd.astype(F32)), jax.nn.log_sigmoid(decay_bwd.astype(F32))]),
        bias_tbl=_na_bias_table(rpb),
        w_ret=w_ret_out.astype(BF16), w_na=w_na_out.astype(BF16), w_mix=w_mix_out.astype(BF16),
        ln1_g=ln1_g.astype(F32).reshape(1, D_MODEL), ln1_b=ln1_b.astype(F32).reshape(1, D_MODEL),
        wr=wr.astype(BF16), br=br.reshape(1, ROUTER_LANES),
        w1=w1.astype(BF16), w3=w3.astype(BF16), w2=w2.astype(BF16),
        ln2_g=ln2_g.astype(F32).reshape(1, D_MODEL), ln2_b=ln2_b.astype(F32).reshape(1, D_MODEL),
    )


def _encoder_layer(x2, p, batch, seq_len, cos, sin, alpha, chunk):
    qk, rest = _in_proj(x2, p["w_in"], cos, sin, seq_len)
    oa = _ret_fwd(qk, rest, p["lg"], batch, seq_len, chunk)
    o_ret = _ret_bwd(qk, rest, oa, p["lg"], batch, seq_len, chunk)
    na = _na(rest, p["bias_tbl"], batch, seq_len)
    x1e = _post(x2, o_ret, na, rest, p["w_ret"], p["w_na"], p["w_mix"], p["ln1_g"], p["ln1_b"], p["wr"], p["br"],
                alpha)
    return _moe(x1e, p["w1"], p["w3"], p["w2"], p["ln2_g"], p["ln2_b"], alpha)


def _trunk(x, layers, alpha, cos, sin, chunk=RET_CHUNK):
    batch, seq_len, _ = x.shape
    x2 = x.reshape(batch * seq_len, D_MODEL)
    for p in layers:
        x2 = _encoder_layer(x2, p, batch, seq_len, cos, sin, alpha, chunk)
    return x2.reshape(batch, seq_len, D_MODEL)


def kernel(x_prompt, x_sample, w_in, ret_decay_fwd, ret_decay_bwd, na_rel_bias, w_ret_out, w_na_out, w_mix_out, ln1_g,
           ln1_b, router_group_w, router_group_b, router_expert_w, router_expert_b, expert_w1, expert_w3, expert_w2,
           ln2_g, ln2_b):
    depth = w_in.shape[0]
    alpha = (2 * depth) ** 0.25
    per_layer = (w_in, ret_decay_fwd, ret_decay_bwd, na_rel_bias, w_ret_out, w_na_out, w_mix_out, ln1_g, ln1_b,
                 router_group_w, router_group_b, router_expert_w, router_expert_b, expert_w1, expert_w3, expert_w2,
                 ln2_g, ln2_b)
    layers = [_prepare_layer(*[a[l] for a in per_layer]) for l in range(depth)]
    cos, sin = _rotary_tables(max(x_prompt.shape[1], x_sample.shape[1]))
    return (_trunk(x_prompt, layers, alpha, cos, sin), _trunk(x_sample, layers, alpha, cos, sin))
```
